```python
import jax, jax.numpy as jnp
from jax import lax
import numpy as np

D_MODEL = 2048
BATCH = 4
SEQ = 4096
DEPTH = 2

GRID_W = 64
CTX_LEN = 256
EPS = 1e-6
N_MOD = 6

A_GROUPS = 8
A_GROUP_W = 128
A_WIDTH = A_GROUPS * A_GROUP_W
A_CHUNK = 128

GLA_HEADS = 4
GLA_DK = D_MODEL // 2
GLA_DV = D_MODEL
GLA_DK_HEAD = GLA_DK // GLA_HEADS
GLA_DV_HEAD = GLA_DV // GLA_HEADS
GLA_RANK = 16
GLA_TAU = 16.0
GLA_CHUNK = 64
CONV_W = 3

N_EXPERTS = 32
TOP_K = 4
D_EXPERT = D_MODEL
SWIGLU_LIMIT = 7.0
SWIGLU_ALPHA = 1.702
MOE_BLOCK = 128

PROJ_SIZES = (A_WIDTH, A_WIDTH,
              GLA_DK, GLA_DK, GLA_DV,
              GLA_DV,
              GLA_RANK, GLA_RANK,
              D_MODEL, D_MODEL)
IN_COLS = sum(PROJ_SIZES)
CONV_CH = 2 * GLA_DK + GLA_DV

kernel_name = "hybrid_gmlp_gla_moe_dit_prefix"


def _silu(t):
    return t * jax.nn.sigmoid(t)


def _rmsnorm(t, g):
    t32 = t.astype(jnp.float32)
    t32 = t32 * lax.rsqrt(jnp.mean(t32 * t32, axis=-1, keepdims=True) + EPS)
    return t32.astype(t.dtype) * g


def _layernorm(t):
    t32 = t.astype(jnp.float32)
    mu = jnp.mean(t32, axis=-1, keepdims=True)
    var = jnp.mean(jnp.square(t32 - mu), axis=-1, keepdims=True)
    return ((t32 - mu) * lax.rsqrt(var + EPS)).astype(t.dtype)


def _modulate(t, shift, scale):
    return t * (1.0 + scale) + shift


def _split_proj(p):
    cuts, acc = [], 0
    for s in PROJ_SIZES[:-1]:
        acc += s
        cuts.append(acc)
    return jnp.split(p, cuts, axis=-1)


def _heads(t, d):
    b, n, _ = t.shape
    return t.reshape(b, n, GLA_HEADS, d).transpose(0, 2, 1, 3)


def _short_conv(t, w, rows):
    shape = t.shape
    if rows is not None:
        t = t.reshape(shape[0], rows, GRID_W, shape[-1])
    half = CONV_W // 2
    n = t.shape[-2]
    tp = jnp.pad(t, [(0, 0)] * (t.ndim - 2) + [(half, half), (0, 0)])
    y = sum(w[j] * tp[..., j:j + n, :] for j in range(CONV_W))
    return y.reshape(shape)


def _chunk_mlp(u, v, ws, bias):
    bsz, n_tok, _ = u.shape
    n_chunk = n_tok // A_CHUNK
    v = _layernorm(v).reshape(bsz, n_chunk, A_CHUNK, A_GROUPS, A_GROUP_W)
    s = jnp.einsum("gts,bnsgc->bntgc", ws, v) + bias.T[:, :, None]
    return u * s.reshape(bsz, n_tok, A_WIDTH)


def _gla_q(q_raw, conv_w, rows):
    q = _silu(_short_conv(q_raw, conv_w[:, :GLA_DK], rows)) * (GLA_DK_HEAD ** -0.5)
    return _heads(q, GLA_DK_HEAD)


def _gla_kvg(k_raw, v_raw, lr_f, lr_b, conv_w, wf, bf, wb, bb, rows):
    k = _silu(_short_conv(k_raw, conv_w[:, GLA_DK:2 * GLA_DK], rows))
    v = _silu(_short_conv(v_raw, conv_w[:, 2 * GLA_DK:], rows))
    gf = jax.nn.log_sigmoid((lr_f @ wf + bf).astype(jnp.float32)) / GLA_TAU
    gb = jax.nn.log_sigmoid((lr_b @ wb + bb).astype(jnp.float32)) / GLA_TAU
    return (_heads(k, GLA_DK_HEAD), _heads(v, GLA_DV_HEAD),
            _heads(gf, GLA_DK_HEAD), _heads(gb, GLA_DK_HEAD))


def _gla_scan(q, k, v, g, s0):
    b_, h_, n_tok, _ = q.shape
    n_chunk = n_tok // GLA_CHUNK
    rs = lambda t: t.astype(jnp.float32).reshape(b_, h_, n_chunk, GLA_CHUNK, t.shape[-1])
    q, k, v, g = rs(q), rs(k), rs(v), rs(g)
    cum = jnp.cumsum(g, axis=3)
    cum_last = cum[:, :, :, -1:, :]
    q_t = q * jnp.exp(cum)
    k_t = k * jnp.exp(-cum)
    k_end = k * jnp.exp(cum_last - cum)
    mask = jnp.tril(jnp.ones((GLA_CHUNK, GLA_CHUNK), dtype=bool))
    attn = jnp.where(mask, jnp.einsum("bhntk,bhnsk->bhnts", q_t, k_t), 0.0)
    o_intra = jnp.einsum("bhnts,bhnsv->bhntv", attn, v)

    def step(state, xs):
        qn, kn, vn, dn = xs
        o = jnp.einsum("bhtk,bhkv->bhtv", qn, state)
        state = dn[..., None] * state + jnp.einsum("bhtk,bhtv->bhkv", kn, vn)
        return state, o

    xs = (jnp.moveaxis(q_t, 2, 0), jnp.moveaxis(k_end, 2, 0), jnp.moveaxis(v, 2, 0),
          jnp.moveaxis(jnp.exp(cum_last[:, :, :, 0, :]), 2, 0))
    _, o_inter = lax.scan(step, s0.astype(jnp.float32), xs)
    o = o_intra + jnp.moveaxis(o_inter, 0, 2)
    return o.reshape(b_, h_, n_tok, v.shape[-1])


def _gla_final_state(k, g, v):
    cum = jnp.cumsum(g, axis=2)
    w = jnp.exp(cum[:, :, -1:, :] - cum)
    return jnp.einsum("bhtk,bhtv->bhkv", k.astype(jnp.float32) * w, v.astype(jnp.float32))


def _gla_bidir(q, k, v, gf, gb, s0f, s0b, norm_g):
    flip = lambda t: jnp.flip(t, axis=2)
    o = _gla_scan(q, k, v, gf, s0f) + flip(_gla_scan(flip(q), flip(k), flip(v), flip(gb), s0b))
    o = _rmsnorm(o, norm_g)
    b_, h_, n_tok, dv = o.shape
    return o.transpose(0, 2, 1, 3).reshape(b_, n_tok, h_ * dv).astype(q.dtype)


def _mixer_out(u, va, y_gla, r, g_a, g_b, a_ws, a_bias, p_a, p_b, w_out):
    ya = _chunk_mlp(jax.nn.gelu(u), jax.nn.gelu(va), a_ws, a_bias) @ p_a
    yb = (y_gla * _silu(r)) @ p_b
    return (jax.nn.sigmoid(g_a) * ya + jax.nn.sigmoid(g_b) * yb) @ w_out


def _moe(h, w_router, b_router, w1, b1, w2, b2):
    n_tok, d = h.shape
    n_assign = n_tok * TOP_K
    logits = (h @ w_router + b_router).astype(jnp.float32)
    top_val, top_idx = lax.top_k(logits, TOP_K)
    gate = jax.nn.softmax(top_val, axis=-1).astype(h.dtype)
    e_flat = top_idx.reshape(-1)
    tok = jnp.arange(n_assign) // TOP_K
    order = jnp.argsort(e_flat, stable=True)
    e_s, tok_s, gate_s = e_flat[order], tok[order], gate.reshape(-1)[order]
    counts = jnp.bincount(e_flat, length=N_EXPERTS)
    starts = jnp.cumsum(counts) - counts
    padded = (counts + MOE_BLOCK - 1) // MOE_BLOCK * MOE_BLOCK
    p_ends = jnp.cumsum(padded)
    p_starts = p_ends - padded
    dest = p_starts[e_s] + jnp.arange(n_assign) - starts[e_s]
    n_blocks = -(-n_assign // MOE_BLOCK) + N_EXPERTS
    buf = jnp.zeros((n_blocks * MOE_BLOCK, d), h.dtype).at[dest].set(h[tok_s])
    block_e = jnp.minimum(jnp.searchsorted(p_ends, jnp.arange(n_blocks) * MOE_BLOCK, side="right"),
                          N_EXPERTS - 1)

    def expert_block(args):
        xb, e = args
        hb = xb @ w1[e] + b1[e]
        x_glu, x_lin = jnp.split(hb, 2, axis=-1)
        x_glu = jnp.minimum(x_glu, SWIGLU_LIMIT)
        x_lin = jnp.clip(x_lin, -SWIGLU_LIMIT, SWIGLU_LIMIT)
        act = x_glu * jax.nn.sigmoid(SWIGLU_ALPHA * x_glu) * (x_lin + 1.0)
        return act @ w2[e] + b2[e]

    out = lax.map(expert_block, (buf.reshape(n_blocks, MOE_BLOCK, d), block_e))
    y_s = out.reshape(-1, d)[dest] * gate_s[:, None]
    return jnp.zeros((n_tok, d), h.dtype).at[tok_s].add(y_s)


def setup_inputs(seed: int = 0) -> dict:
    key = jax.random.key(seed)
    ks = jax.random.split(key, 32)

    def nrm(i, shape, scale):
        return jax.random.normal(ks[i], shape, jnp.float32) * scale

    d, f = D_MODEL, D_EXPERT
    return {
        "x": nrm(0, (BATCH, SEQ, d), 1.0),
        "c": nrm(1, (BATCH, d), 1.0),
        "ctx": nrm(2, (BATCH, CTX_LEN, d), 1.0),
        "c_ctx": nrm(3, (d,), 1.0),
        "norm_mix": 1.0 + nrm(4, (DEPTH, d), 0.02),
        "norm_ffn": 1.0 + nrm(5, (DEPTH, d), 0.02),
        "w_ada": nrm(6, (DEPTH, d, N_MOD * d), d ** -0.5),
        "b_ada": nrm(7, (DEPTH, N_MOD * d), 0.02),
        "w_in": nrm(8, (DEPTH, d, IN_COLS), d ** -0.5),
        "conv_w": nrm(9, (DEPTH, CONV_W, CONV_CH), CONV_W ** -0.5),
        "a_ws": nrm(10, (DEPTH, A_GROUPS, A_CHUNK, A_CHUNK), A_CHUNK ** -0.5),
        "a_bias": 1.0 + nrm(11, (DEPTH, A_GROUPS, A_CHUNK), 0.02),
        "gla_wf": nrm(12, (DEPTH, GLA_RANK, GLA_DK), GLA_RANK ** -0.5),
        "gla_bf": nrm(13, (DEPTH, GLA_DK), 0.02),
        "gla_wb": nrm(14, (DEPTH, GLA_RANK, GLA_DK), GLA_RANK ** -0.5),
        "gla_bb": nrm(15, (DEPTH, GLA_DK), 0.02),
        "gla_norm": 1.0 + nrm(16, (DEPTH, GLA_DV_HEAD), 0.02),
        "p_a": nrm(17, (DEPTH, A_WIDTH, d), A_WIDTH ** -0.5),
        "p_b": nrm(18, (DEPTH, GLA_DV, d), GLA_DV ** -0.5),
        "w_out": nrm(19, (DEPTH, d, d), d ** -0.5),
        "w_router": nrm(20, (DEPTH, d, N_EXPERTS), d ** -0.5),
        "b_router": nrm(21, (DEPTH, N_EXPERTS), 0.01),
        "w1": nrm(22, (DEPTH, N_EXPERTS, d, 2 * f), d ** -0.5),
        "b1": nrm(23, (DEPTH, N_EXPERTS, 2 * f), 0.02),
        "w2": nrm(24, (DEPTH, N_EXPERTS, f, d), f ** -0.5),
        "b2": nrm(25, (DEPTH, N_EXPERTS, d), 0.02),
        "norm_final": 1.0 + nrm(26, (d,), 0.02),
    }


def reference(x, c, ctx, c_ctx, norm_mix, norm_ffn, w_ada, b_ada, w_in, conv_w, a_ws, a_bias,
              gla_wf, gla_bf, gla_wb, gla_bb, gla_norm, p_a, p_b, w_out, w_router, b_router,
              w1, b1, w2, b2, norm_final):
    d = x.shape[-1]
    rows = x.shape[1] // GRID_W
    for l in range(DEPTH):
        last = l == DEPTH - 1
        m = _silu(c) @ w_ada[l] + b_ada[l]
        mc = _silu(c_ctx) @ w_ada[l] + b_ada[l]
        sh_m, sc_m, gt_m, sh_f, sc_f, gt_f = [t[:, None, :] for t in jnp.split(m, N_MOD, axis=-1)]
        csh_m, csc_m, cgt_m, csh_f, csc_f, cgt_f = jnp.split(mc, N_MOD, axis=-1)

        h = _modulate(_rmsnorm(x, norm_mix[l]), sh_m, sc_m)
        hc = _modulate(_rmsnorm(ctx, norm_mix[l]), csh_m, csc_m)
        u, va, q_raw, k_raw, v_raw, r, lr_f, lr_b, g_a, g_b = _split_proj(h @ w_in[l])
        cu, cva, cq_raw, ck_raw, cv_raw, cr, clr_f, clr_b, cg_a, cg_b = _split_proj(hc @ w_in[l])
        gla_p = (conv_w[l], gla_wf[l], gla_bf[l], gla_wb[l], gla_bb[l])
        mix_p = (a_ws[l], a_bias[l], p_a[l], p_b[l], w_out[l])

        kc, vc, gfc, gbc = _gla_kvg(ck_raw, cv_raw, clr_f, clr_b, *gla_p, None)
        s0f = _gla_final_state(kc, gfc, vc)
        s0b = _gla_final_state(jnp.flip(kc, 2), jnp.flip(gbc, 2), jnp.flip(vc, 2))

        k, v, gf, gb = _gla_kvg(k_raw, v_raw, lr_f, lr_b, *gla_p, rows)
        q = _gla_q(q_raw, conv_w[l], rows)
        y_gla = _gla_bidir(q, k, v, gf, gb, s0f, s0b, gla_norm[l])
        x_new = x + gt_m * _mixer_out(u, va, y_gla, r, g_a, g_b, *mix_p)

        if not last:
            qc = _gla_q(cq_raw, conv_w[l], None)
            zero = jnp.zeros_like(s0f)
            yc_gla = _gla_bidir(qc, kc, vc, gfc, gbc, zero, zero, gla_norm[l])
            ctx = ctx + cgt_m * _mixer_out(cu, cva, yc_gla, cr, cg_a, cg_b, *mix_p)
        x = x_new

        moe_p = (w_router[l], b_router[l], w1[l], b1[l], w2[l], b2[l])
        hf = _modulate(_rmsnorm(x, norm_ffn[l]), sh_f, sc_f).reshape(-1, d)
        if last:
            x = x + gt_f * _moe(hf, *moe_p).reshape(x.shape)
        else:
            hcf = _modulate(_rmsnorm(ctx, norm_ffn[l]), csh_f, csc_f).reshape(-1, d)
            y = _moe(jnp.concatenate([hcf, hf], axis=0), *moe_p)
            n_c = hcf.shape[0]
            ctx = ctx + cgt_f * y[:n_c].reshape(ctx.shape)
            x = x + gt_f * y[n_c:].reshape(x.shape)
    return _rmsnorm(x, norm_final)
```

```python
import functools

import jax
import jax.numpy as jnp
from jax import lax
from jax.experimental import pallas as pl
from jax.experimental.pallas import tpu as pltpu

F32 = jnp.float32
BF16 = jnp.bfloat16

GRID_W = 64
EPS = 1e-6
N_MOD = 6
A_GROUPS = 8
A_GROUP_W = 128
A_CHUNK = 128
GLA_HEADS = 4
GLA_RANK = 16
GLA_TAU = 16.0
GLA_CHUNK = 64
N_EXPERTS = 32
TOP_K = 4
SWIGLU_LIMIT = 7.0
SWIGLU_ALPHA = 1.702

LANES = 128
ROW_TILE = 1024
COL_TILE = 512
GLA_BLOCK = 256
EXPERT_ROWS = 1024
EXPERT_SUB = 256
EXPERT_FT = 256
MIB = 1024 * 1024


def _cparams(vmem_mib, n_axes):
    return pltpu.CompilerParams(
        dimension_semantics=("arbitrary",) * n_axes,
        vmem_limit_bytes=int(vmem_mib * MIB))


def _silu(t):
    return t * jax.nn.sigmoid(t)


def _gelu_tanh(t):
    return 0.5 * t * (1.0 + jnp.tanh(0.7978845608028654 * (t + 0.044715 * (t * t * t))))


def _log_sigmoid(z):
    return jnp.minimum(z, 0.0) - jnp.log1p(jnp.exp(-jnp.abs(z)))


def _ada_kernel(c_ref, w_ref, b_ref, o_ref):
    s = _silu(c_ref[...]).astype(BF16)
    w = w_ref[0].astype(BF16)
    o_ref[0] = jnp.dot(s, w, preferred_element_type=F32) + b_ref[0]


def ada_mods(cond, w_ada, b_ada):
    n_layer, d, n = w_ada.shape
    tn = 1024
    return pl.pallas_call(
        _ada_kernel,
        grid=(n_layer, n // tn),
        in_specs=[pl.BlockSpec((8, d), lambda l, j: (0, 0)),
                  pl.BlockSpec((1, d, tn), lambda l, j: (l, 0, j)),
                  pl.BlockSpec((1, 1, tn), lambda l, j: (l, 0, j))],
        out_specs=pl.BlockSpec((1, 8, tn), lambda l, j: (l, 0, j)),
        out_shape=jax.ShapeDtypeStruct((n_layer, 8, n), F32),
        compiler_params=_cparams(32, 2),
        name="ada_mods",
    )(cond, w_ada, b_ada.reshape(n_layer, 1, n))


def _norm_mod_kernel(x_ref, g_ref, sh_ref, sc_ref, o_ref):
    x = x_ref[...]
    r = lax.rsqrt(jnp.mean(x * x, axis=-1, keepdims=True) + EPS)
    h = (x * r) * g_ref[...]
    h = h * (1.0 + sc_ref[0]) + sh_ref[0]
    o_ref[...] = h.astype(o_ref.dtype)


def norm_mod(xa, gain, mods, sh_row, sc_row, mod_idx):
    t, d = xa.shape
    tm = 512
    per = ROW_TILE // tm
    return pl.pallas_call(
        _norm_mod_kernel,
        grid=(t // tm,),
        in_specs=[pl.BlockSpec((tm, d), lambda i: (i, 0)),
                  pl.BlockSpec((1, d), lambda i: (0, 0)),
                  pl.BlockSpec((1, 1, d), lambda i: (sh_row + mod_idx(i // per), 0, 0)),
                  pl.BlockSpec((1, 1, d), lambda i: (sc_row + mod_idx(i // per), 0, 0))],
        out_specs=pl.BlockSpec((tm, d), lambda i: (i, 0)),
        out_shape=jax.ShapeDtypeStruct((t, d), BF16),
        compiler_params=_cparams(32, 1),
        name="norm_mod",
    )(xa, gain.reshape(1, d), mods, mods)


def _mm_kernel(*refs, act, ctx_seg, q_tiles, q_scale):
    if act == "conv":
        x_ref, w_ref, cw_ref, o_ref, wb_ref = refs
    else:
        x_ref, w_ref, o_ref, wb_ref = refs
    j = pl.program_id(0)
    i = pl.program_id(1)

    @pl.when(i == 0)
    def _():
        wb_ref[...] = w_ref[...].astype(BF16)

    y = jnp.dot(x_ref[...], wb_ref[...], preferred_element_type=F32)
    if act == "gelu":
        y = _gelu_tanh(y)
    elif act == "silu":
        y = _silu(y)
    elif act == "sigmoid":
        y = jax.nn.sigmoid(y)
    elif act == "conv":
        seg = jnp.where(i == 0, ctx_seg, GRID_W)
        pos = lax.broadcasted_iota(jnp.int32, y.shape, 0) & (seg - 1)
        prev = jnp.where(pos == 0, 0.0, pltpu.roll(y, 1, 0))
        nxt = jnp.where(pos == seg - 1, 0.0, pltpu.roll(y, y.shape[0] - 1, 0))
        cw = cw_ref[...]
        y = _silu(cw[0:1] * prev + cw[1:2] * y + cw[2:3] * nxt)
        y = y * jnp.where(j < q_tiles, q_scale, 1.0)
    o_ref[...] = y.astype(o_ref.dtype)


def matmul_act(h, w, layer, col0, ncols, act="none", out_dtype=BF16, conv_w=None,
               ctx_seg=256, q_cols=0, q_scale=1.0, tn=COL_TILE):
    t, k = h.shape
    tm = ROW_TILE
    off = col0 // tn
    in_specs = [pl.BlockSpec((tm, k), lambda j, i: (i, 0)),
                pl.BlockSpec((None, k, tn), lambda j, i: (layer, 0, j + off))]
    args = [h, w]
    if act == "conv":
        in_specs.append(pl.BlockSpec((3, tn), lambda j, i: (0, j)))
        args.append(conv_w)
    return pl.pallas_call(
        functools.partial(_mm_kernel, act=act, ctx_seg=ctx_seg, q_tiles=q_cols // tn,
                          q_scale=q_scale),
        grid=(ncols // tn, t // tm),
        in_specs=in_specs,
        out_specs=pl.BlockSpec((tm, tn), lambda j, i: (i, j)),
        out_shape=jax.ShapeDtypeStruct((t, ncols), out_dtype),
        scratch_shapes=[pltpu.VMEM((k, tn), BF16)],
        compiler_params=_cparams(48, 2),
        name="matmul_" + act,
    )(*args)


def _gla_chunk(q, k, v, g, s_ref, tri, eye, reverse):
    cum = jnp.dot(tri, g, preferred_element_type=F32)
    tot = cum[0:1] if reverse else cum[GLA_CHUNK - 1:GLA_CHUNK]
    qt = (q * jnp.exp(cum)).astype(BF16)
    kt = (k * jnp.exp(-cum)).astype(BF16)
    ke = (k * jnp.exp(tot - cum)).astype(BF16)
    att = lax.dot_general(qt, kt, (((1,), (1,)), ((), ())), preferred_element_type=F32)
    att = jnp.where(tri > 0.0, att, 0.0).astype(BF16)
    s = s_ref[...]
    o = (jnp.dot(att, v, preferred_element_type=F32)
         + jnp.dot(qt, s.astype(BF16), preferred_element_type=F32))
    dk = g.shape[1]
    tot_col = jnp.sum(jnp.where(eye, jnp.broadcast_to(tot, (dk, dk)), 0.0), axis=1, keepdims=True)
    kv = lax.dot_general(ke, v, (((0,), (0,)), ((), ())), preferred_element_type=F32)
    s_ref[...] = s * jnp.exp(tot_col) + kv
    return o


def _gla_kernel(*refs, reverse):
    if reverse:
        (q_ref, k_ref, v_ref, lr_ref, w_ref, b_ref, of_ref, r_ref, ng_ref, y_ref, s_ref) = refs
    else:
        (q_ref, k_ref, v_ref, lr_ref, w_ref, b_ref, o_ref, s_ref) = refs
    step = pl.program_id(2)

    @pl.when(step == 0)
    def _():
        s_ref[...] = jnp.zeros_like(s_ref)

    c = GLA_CHUNK
    dk = q_ref.shape[1]
    z = jnp.dot(lr_ref[...], w_ref[...], preferred_element_type=F32) + b_ref[...]
    g = _log_sigmoid(z) * (1.0 / GLA_TAU)
    row = lax.broadcasted_iota(jnp.int32, (c, c), 0)
    col = lax.broadcasted_iota(jnp.int32, (c, c), 1)
    tri = ((col >= row) if reverse else (col <= row)).astype(F32)
    eye = (lax.broadcasted_iota(jnp.int32, (dk, dk), 0)
           == lax.broadcasted_iota(jnp.int32, (dk, dk), 1))
    n_chunk = q_ref.shape[0] // c
    order = range(n_chunk - 1, -1, -1) if reverse else range(n_chunk)
    for ci in order:
        rows = pl.ds(ci * c, c)
        o = _gla_chunk(q_ref[rows, :].astype(F32), k_ref[rows, :].astype(F32), v_ref[rows, :],
                       g[ci * c:(ci + 1) * c], s_ref, tri, eye, reverse)
        if reverse:
            o = o + of_ref[rows, :]
            o = o * lax.rsqrt(jnp.mean(o * o, axis=-1, keepdims=True) + EPS)
            y_ref[rows, :] = (o * ng_ref[...] * r_ref[rows, :].astype(F32)).astype(y_ref.dtype)
        else:
            o_ref[rows, :] = o


def gla_scan(qkv, lr, r_act, o_fwd, w_pad, bias, norm_g, n_batch, n_ctx_blocks, reverse):
    t = qkv.shape[0]
    dkh = w_pad.shape[1] // GLA_HEADS
    dv = qkv.shape[1] - 2 * w_pad.shape[1]
    dvh = dv // GLA_HEADS
    blk = GLA_BLOCK
    ctx_per_batch = n_ctx_blocks // n_batch
    assert ctx_per_batch == 1
    lat_per_batch = (t // blk - n_ctx_blocks) // n_batch
    n_step = ctx_per_batch + lat_per_batch

    def row_blk(b, s):
        if reverse:
            lat = n_ctx_blocks + lat_per_batch * b + (lat_per_batch - s)
        else:
            lat = n_ctx_blocks + lat_per_batch * b + (s - 1)
        return jnp.where(s == 0, b, lat)

    k_off = GLA_HEADS
    v_off = 2 * w_pad.shape[1] // dvh
    in_specs = [pl.BlockSpec((blk, dkh), lambda b, h, s: (row_blk(b, s), h)),
                pl.BlockSpec((blk, dkh), lambda b, h, s: (row_blk(b, s), k_off + h)),
                pl.BlockSpec((blk, dvh), lambda b, h, s: (row_blk(b, s), v_off + h)),
                pl.BlockSpec((blk, LANES), lambda b, h, s: (row_blk(b, s), 0)),
                pl.BlockSpec((LANES, dkh), lambda b, h, s: (0, h)),
                pl.BlockSpec((1, dkh), lambda b, h, s: (0, h))]
    args = [qkv, qkv, qkv, lr, w_pad, bias]
    if reverse:
        in_specs += [pl.BlockSpec((blk, dvh), lambda b, h, s: (row_blk(b, s), h)),
                     pl.BlockSpec((blk, dvh), lambda b, h, s: (row_blk(b, s), h)),
                     pl.BlockSpec((1, dvh), lambda b, h, s: (0, 0))]
        args += [o_fwd, r_act, norm_g]
        out_dtype = BF16
    else:
        out_dtype = F32
    return pl.pallas_call(
        functools.partial(_gla_kernel, reverse=reverse),
        grid=(n_batch, GLA_HEADS, n_step),
        in_specs=in_specs,
        out_specs=pl.BlockSpec((blk, dvh), lambda b, h, s: (row_blk(b, s), h)),
        out_shape=jax.ShapeDtypeStruct((t, dv), out_dtype),
        scratch_shapes=[pltpu.VMEM((dkh, dvh), F32)],
        compiler_params=_cparams(32, 3),
        name="gla_bwd" if reverse else "gla_fwd",
    )(*args)


def _gmlp_kernel(u_ref, v_ref, ws_ref, bt_ref, o_ref):
    v = v_ref[...].astype(F32)
    mu = jnp.mean(v, axis=-1, keepdims=True)
    vc = v - mu
    var = jnp.mean(vc * vc, axis=-1, keepdims=True)
    vn = (vc * lax.rsqrt(var + EPS)).astype(BF16)
    n_chunk = v.shape[0] // A_CHUNK
    for gi in range(A_GROUPS):
        wsg = ws_ref[gi].astype(BF16)
        bias = bt_ref[:, gi:gi + 1]
        cols = slice(gi * A_GROUP_W, (gi + 1) * A_GROUP_W)
        for n in range(n_chunk):
            rows = slice(n * A_CHUNK, (n + 1) * A_CHUNK)
            s = jnp.dot(wsg, vn[rows, cols], preferred_element_type=F32) + bias
            o_ref[rows, cols] = (u_ref[rows, cols].astype(F32) * s).astype(o_ref.dtype)


def gmlp(uv, a_ws, a_bias_t):
    t = uv.shape[0]
    aw = uv.shape[1] // 2
    tm = 512
    return pl.pallas_call(
        _gmlp_kernel,
        grid=(t // tm,),
        in_specs=[pl.BlockSpec((tm, aw), lambda i: (i, 0)),
                  pl.BlockSpec((tm, aw), lambda i: (i, 1)),
                  pl.BlockSpec(a_ws.shape, lambda i: (0, 0, 0)),
                  pl.BlockSpec(a_bias_t.shape, lambda i: (0, 0))],
        out_specs=pl.BlockSpec((tm, aw), lambda i: (i, 0)),
        out_shape=jax.ShapeDtypeStruct((t, aw), BF16),
        compiler_params=_cparams(32, 1),
        name="gmlp",
    )(uv, uv, a_ws, a_bias_t)


def _merge_kernel(a_ref, y_ref, ga_ref, gb_ref, pa_ref, pb_ref, o_ref, pab_ref, pbb_ref):
    @pl.when(pl.program_id(1) == 0)
    def _():
        pab_ref[...] = pa_ref[...].astype(BF16)
        pbb_ref[...] = pb_ref[...].astype(BF16)

    ya = jnp.dot(a_ref[...], pab_ref[...], preferred_element_type=F32)
    yb = jnp.dot(y_ref[...], pbb_ref[...], preferred_element_type=F32)
    o_ref[...] = (ga_ref[...].astype(F32) * ya + gb_ref[...].astype(F32) * yb).astype(o_ref.dtype)


def merge_proj(a, yb, gates, p_a, p_b, layer):
    t, d = yb.shape
    tm, tn = ROW_TILE, COL_TILE
    nj = d // tn
    ka, kb = p_a.shape[1], p_b.shape[1]
    return pl.pallas_call(
        _merge_kernel,
        grid=(nj, t // tm),
        in_specs=[pl.BlockSpec((tm, ka), lambda j, i: (i, 0)),
                  pl.BlockSpec((tm, kb), lambda j, i: (i, 0)),
                  pl.BlockSpec((tm, tn), lambda j, i: (i, j)),
                  pl.BlockSpec((tm, tn), lambda j, i: (i, nj + j)),
                  pl.BlockSpec((None, ka, tn), lambda j, i: (layer, 0, j)),
                  pl.BlockSpec((None, kb, tn), lambda j, i: (layer, 0, j))],
        out_specs=pl.BlockSpec((tm, tn), lambda j, i: (i, j)),
        out_shape=jax.ShapeDtypeStruct((t, d), BF16),
        scratch_shapes=[pltpu.VMEM((ka, tn), BF16), pltpu.VMEM((kb, tn), BF16)],
        compiler_params=_cparams(48, 2),
        name="merge_proj",
    )(a, yb, gates, gates, p_a, p_b)


def _out_proj_kernel(m_ref, w_ref, x_ref, gt_ref, o_ref, wb_ref):
    @pl.when(pl.program_id(1) == 0)
    def _():
        wb_ref[...] = w_ref[...].astype(BF16)

    y = jnp.dot(m_ref[...], wb_ref[...], preferred_element_type=F32)
    o_ref[...] = x_ref[...] + gt_ref[0] * y


def out_proj(mrg, w_out, layer, xa, mods, gt_row, mod_idx):
    t, d = xa.shape
    tm, tn = ROW_TILE, COL_TILE
    return pl.pallas_call(
        _out_proj_kernel,
        grid=(d // tn, t // tm),
        in_specs=[pl.BlockSpec((tm, d), lambda j, i: (i, 0)),
                  pl.BlockSpec((None, d, tn), lambda j, i: (layer, 0, j)),
                  pl.BlockSpec((tm, tn), lambda j, i: (i, j)),
                  pl.BlockSpec((1, 1, tn), lambda j, i: (gt_row + mod_idx(i), 0, j))],
        out_specs=pl.BlockSpec((tm, tn), lambda j, i: (i, j)),
        out_shape=jax.ShapeDtypeStruct((t, d), F32),
        scratch_shapes=[pltpu.VMEM((d, tn), BF16)],
        compiler_params=_cparams(48, 2),
        name="out_proj",
    )(mrg, w_out, xa, mods)


def _router_kernel(x_ref, g_ref, sh_ref, sc_ref, wr_ref, br_ref,
                   h_ref, idx_ref, gate_ref, rank_ref, cnt_ref, carry_ref):
    @pl.when(pl.program_id(0) == 0)
    def _():
        carry_ref[...] = jnp.zeros_like(carry_ref)

    x = x_ref[...]
    r = lax.rsqrt(jnp.mean(x * x, axis=-1, keepdims=True) + EPS)
    h = (x * r) * g_ref[...]
    h = h * (1.0 + sc_ref[0]) + sh_ref[0]
    h_ref[...] = h
    logits = jnp.dot(h, wr_ref[...], preferred_element_type=F32) + br_ref[...]
    tm = x.shape[0]
    lane = lax.broadcasted_iota(jnp.int32, (tm, LANES), 1)
    lane_f = lane.astype(F32)
    member = jnp.zeros((tm, LANES), F32)
    vals, sels = [], []
    idx_out = jnp.zeros((tm, LANES), F32)
    for kk in range(TOP_K):
        m = jnp.max(logits, axis=1, keepdims=True)
        ik = jnp.min(jnp.where(logits == m, lane_f, float(LANES)), axis=1, keepdims=True)
        sel = lane_f == ik
        logits = jnp.where(sel, -jnp.inf, logits)
        member = member + sel.astype(F32)
        idx_out = jnp.where(lane == kk, ik, idx_out)
        vals.append(m)
        sels.append(sel)
    es = [jnp.exp(v - vals[0]) for v in vals]
    denom = es[0] + es[1] + es[2] + es[3]
    gate_out = jnp.zeros((tm, LANES), F32)
    for kk in range(TOP_K):
        gate_out = jnp.where(lane == kk, es[kk] / denom, gate_out)
    rr = lax.broadcasted_iota(jnp.int32, (tm, tm), 0)
    cc = lax.broadcasted_iota(jnp.int32, (tm, tm), 1)
    strict = (cc < rr).astype(BF16)
    before = jnp.dot(strict, member.astype(BF16), preferred_element_type=F32) + carry_ref[...]
    rank_out = jnp.zeros((tm, LANES), F32)
    for kk in range(TOP_K):
        rk = jnp.sum(jnp.where(sels[kk], before, 0.0), axis=1, keepdims=True)
        rank_out = jnp.where(lane == kk, rk, rank_out)
    carry_ref[...] = carry_ref[...] + jnp.sum(member, axis=0, keepdims=True)
    idx_ref[...] = idx_out.astype(jnp.int32)
    gate_ref[...] = gate_out
    rank_ref[...] = rank_out.astype(jnp.int32)
    cnt_ref[...] = carry_ref[...]


def router(xa, row0_tiles, gain, mods, sh_row, sc_row, mod_idx, wr_pad, br_pad):
    t, d = xa.shape
    tm = 512
    per = ROW_TILE // tm
    off = row0_tiles * per
    n = t // tm - off
    tl = n * tm
    small = lambda dt: jax.ShapeDtypeStruct((tl, LANES), dt)
    return pl.pallas_call(
        _router_kernel,
        grid=(n,),
        in_specs=[pl.BlockSpec((tm, d), lambda i: (i + off, 0)),
                  pl.BlockSpec((1, d), lambda i: (0, 0)),
                  pl.BlockSpec((1, 1, d), lambda i: (sh_row + mod_idx((i + off) // per), 0, 0)),
                  pl.BlockSpec((1, 1, d), lambda i: (sc_row + mod_idx((i + off) // per), 0, 0)),
                  pl.BlockSpec((d, LANES), lambda i: (0, 0)),
                  pl.BlockSpec((1, LANES), lambda i: (0, 0))],
        out_specs=[pl.BlockSpec((tm, d), lambda i: (i, 0)),
                   pl.BlockSpec((tm, LANES), lambda i: (i, 0)),
                   pl.BlockSpec((tm, LANES), lambda i: (i, 0)),
                   pl.BlockSpec((tm, LANES), lambda i: (i, 0)),
                   pl.BlockSpec((1, LANES), lambda i: (0, 0))],
        out_shape=[jax.ShapeDtypeStruct((tl, d), F32), small(jnp.int32), small(F32),
                   small(jnp.int32), jax.ShapeDtypeStruct((1, LANES), F32)],
        scratch_shapes=[pltpu.VMEM((1, LANES), F32)],
        compiler_params=_cparams(40, 1),
        name="router",
    )(xa, gain.reshape(1, d), mods, mods, wr_pad, br_pad)


DMA_UNROLL = 8


def _row_gather(idx_ref, n_rows, src_ref, dst_ref, sem):
    def body(g, carry):
        for u in range(DMA_UNROLL):
            r = g * DMA_UNROLL + u
            pltpu.make_async_copy(src_ref.at[pl.ds(idx_ref[0, 0, r], 1)],
                                  dst_ref.at[pl.ds(r, 1)], sem).start()
        return carry
    lax.fori_loop(0, n_rows // DMA_UNROLL, body, 0)


def _dispatch_kernel(na_ref, src_ref, h_ref, o_ref, buf_ref, sem):
    b = pl.program_id(0)
    rows = buf_ref.shape[0]

    @pl.when(b < na_ref[0])
    def _():
        _row_gather(src_ref, rows, h_ref, buf_ref, sem)
        pltpu.make_async_copy(h_ref.at[pl.ds(0, rows)], buf_ref, sem).wait()
        o_ref[...] = buf_ref[...].astype(o_ref.dtype)

    @pl.when(b >= na_ref[0])
    def _():
        o_ref[...] = jnp.zeros_like(o_ref)


def dispatch(h, src_tok, n_active):
    n_blk = src_tok.shape[0]
    d = h.shape[1]
    rows = EXPERT_ROWS
    return pl.pallas_call(
        _dispatch_kernel,
        grid_spec=pltpu.PrefetchScalarGridSpec(
            num_scalar_prefetch=1,
            grid=(n_blk,),
            in_specs=[pl.BlockSpec((1, 1, rows), lambda b, na: (b, 0, 0), memory_space=pltpu.SMEM),
                      pl.BlockSpec(memory_space=pl.ANY)],
            out_specs=pl.BlockSpec((rows, d), lambda b, na: (b, 0)),
            scratch_shapes=[pltpu.VMEM((rows, d), F32), pltpu.SemaphoreType.DMA(())]),
        out_shape=jax.ShapeDtypeStruct((n_blk * rows, d), BF16),
        compiler_params=_cparams(32, 1),
        name="moe_dispatch",
    )(n_active, src_tok, h)


def _expert_kernel(be_ref, bv_ref, na_ref, x_ref, w1g_ref, w1l_ref, b1g_ref, b1l_ref,
                   w2_ref, b2_ref, o_ref, wg_s, wl_s, w2_s):
    b = pl.program_id(0)
    j = pl.program_id(1)
    active = b < na_ref[0]

    @pl.when(jnp.logical_and(jnp.logical_not(active), j == 0))
    def _():
        o_ref[...] = jnp.zeros_like(o_ref)

    @pl.when(active)
    def _():
        wg_s[...] = w1g_ref[0].astype(BF16)
        wl_s[...] = w1l_ref[0].astype(BF16)
        w2_s[...] = w2_ref[0].astype(BF16)
        valid = bv_ref[b]
        for s in range(EXPERT_ROWS // EXPERT_SUB):
            rows = pl.ds(s * EXPERT_SUB, EXPERT_SUB)

            @pl.when(s * EXPERT_SUB < valid)
            def _():
                xs = x_ref[rows, :]
                hg = jnp.dot(xs, wg_s[...], preferred_element_type=F32) + b1g_ref[0]
                hl = jnp.dot(xs, wl_s[...], preferred_element_type=F32) + b1l_ref[0]
                hg = jnp.minimum(hg, SWIGLU_LIMIT)
                hl = jnp.clip(hl, -SWIGLU_LIMIT, SWIGLU_LIMIT)
                act = hg * jax.nn.sigmoid(SWIGLU_ALPHA * hg) * (hl + 1.0)
                y = jnp.dot(act.astype(BF16), w2_s[...], preferred_element_type=F32)

                @pl.when(j == 0)
                def _():
                    o_ref[rows, :] = y + b2_ref[0]

                @pl.when(j > 0)
                def _():
                    o_ref[rows, :] += y

            @pl.when(jnp.logical_and(s * EXPERT_SUB >= valid, j == 0))
            def _():
                o_ref[rows, :] = jnp.zeros((EXPERT_SUB, o_ref.shape[1]), o_ref.dtype)


def experts(xs, blk_e, blk_valid, n_active, w1, b1, w2, b2, layer):
    _, n_exp, d, f2 = w1.shape
    f = f2 // 2
    tf = EXPERT_FT
    nf = f // tf
    n_blk = blk_e.shape[0]
    rows = EXPERT_ROWS

    def eb(b, be, na):
        return be[jnp.minimum(b, na[0] - 1)]

    def jf(b, j, na):
        return jnp.where(b < na[0], j, nf - 1)

    in_specs = [
        pl.BlockSpec((rows, d), lambda b, j, be, bv, na: (jnp.minimum(b, na[0] - 1), 0)),
        pl.BlockSpec((None, 1, d, tf), lambda b, j, be, bv, na: (layer, eb(b, be, na), 0, jf(b, j, na))),
        pl.BlockSpec((None, 1, d, tf), lambda b, j, be, bv, na: (layer, eb(b, be, na), 0, nf + jf(b, j, na))),
        pl.BlockSpec((1, 1, tf), lambda b, j, be, bv, na: (eb(b, be, na), 0, jf(b, j, na))),
        pl.BlockSpec((1, 1, tf), lambda b, j, be, bv, na: (eb(b, be, na), 0, nf + jf(b, j, na))),
        pl.BlockSpec((None, 1, tf, d), lambda b, j, be, bv, na: (layer, eb(b, be, na), jf(b, j, na), 0)),
        pl.BlockSpec((1, 1, d), lambda b, j, be, bv, na: (eb(b, be, na), 0, 0)),
    ]
    return pl.pallas_call(
        _expert_kernel,
        grid_spec=pltpu.PrefetchScalarGridSpec(
            num_scalar_prefetch=3,
            grid=(n_blk, nf),
            in_specs=in_specs,
            out_specs=pl.BlockSpec((rows, d), lambda b, j, be, bv, na: (b, 0)),
            scratch_shapes=[pltpu.VMEM((d, tf), BF16), pltpu.VMEM((d, tf), BF16),
                            pltpu.VMEM((tf, d), BF16)]),
        out_shape=jax.ShapeDtypeStruct((n_blk * rows, d), F32),
        compiler_params=_cparams(52, 2),
        name="moe_experts",
    )(blk_e, blk_valid, n_active, xs, w1, w1, b1.reshape(n_exp, 1, f2), b1.reshape(n_exp, 1, f2),
      w2, b2.reshape(n_exp, 1, d))


def _combine_kernel(dest_ref, y_ref, x_ref, gate_ref, gt_ref, ng_ref, o_ref, buf_ref, sem,
                    *, final_norm):
    tm = x_ref.shape[0]
    _row_gather(dest_ref, tm * TOP_K, y_ref, buf_ref, sem)
    pltpu.make_async_copy(y_ref.at[pl.ds(0, tm * TOP_K)], buf_ref, sem).wait()
    gate = gate_ref[...]
    acc = jnp.zeros(x_ref.shape, F32)
    for kk in range(TOP_K):
        acc = acc + gate[:, kk:kk + 1] * buf_ref[pl.ds(kk * tm, tm), :]
    xn = x_ref[...] + gt_ref[0] * acc
    if final_norm:
        xn = xn * lax.rsqrt(jnp.mean(xn * xn, axis=-1, keepdims=True) + EPS) * ng_ref[...]
    o_ref[...] = xn


def combine(y_sorted, dest_km, xa, row0_tiles, gate, mods, gt_row, mod_idx, norm_g, final_norm):
    t, d = xa.shape
    tm = 256
    per = ROW_TILE // tm
    off = row0_tiles * per
    n = t // tm - off
    out_rows = n * tm if final_norm else t
    out_off = 0 if final_norm else off
    kernel = functools.partial(_combine_kernel, final_norm=final_norm)
    call = pl.pallas_call(
        kernel,
        grid=(n,),
        in_specs=[pl.BlockSpec((1, 1, TOP_K * tm), lambda i: (i, 0, 0), memory_space=pltpu.SMEM),
                  pl.BlockSpec(memory_space=pl.ANY),
                  pl.BlockSpec((tm, d), lambda i: (i + off, 0)),
                  pl.BlockSpec((tm, LANES), lambda i: (i, 0)),
                  pl.BlockSpec((1, 1, d), lambda i: (gt_row + mod_idx((i + off) // per), 0, 0)),
                  pl.BlockSpec((1, d), lambda i: (0, 0))],
        out_specs=pl.BlockSpec((tm, d), lambda i: (i + out_off, 0)),
        out_shape=jax.ShapeDtypeStruct((out_rows, d), F32),
        scratch_shapes=[pltpu.VMEM((TOP_K * tm, d), F32), pltpu.SemaphoreType.DMA(())],
        compiler_params=_cparams(32, 1),
        name="moe_combine",
    )
    return call(dest_km, y_sorted, xa, gate, mods, norm_g.reshape(1, d))


def _routing_tables(eidx, rank, counts, n_blk):
    t = eidx.shape[0]
    rows = EXPERT_ROWS
    padded = (counts + rows - 1) // rows * rows
    p_ends = jnp.cumsum(padded)
    p_starts = p_ends - padded
    starts = jnp.cumsum(counts) - counts
    dest = p_starts[eidx] + rank
    n_active = (p_ends[-1] // rows).astype(jnp.int32).reshape(1)
    blk_start = jnp.arange(n_blk, dtype=jnp.int32) * rows
    blk_e = jnp.minimum(jnp.searchsorted(p_ends, blk_start, side="right"), N_EXPERTS - 1).astype(jnp.int32)
    blk_valid = jnp.clip(counts[blk_e] - (blk_start - p_starts[blk_e]), 0, rows).astype(jnp.int32)
    order = jnp.argsort(eidx.reshape(-1), stable=True).astype(jnp.int32)
    tok_sorted = order // TOP_K
    slot = jnp.arange(n_blk * rows, dtype=jnp.int32)
    slot_e = jnp.repeat(blk_e, rows)
    within = slot - p_starts[slot_e]
    src = jnp.where(within < counts[slot_e],
                    tok_sorted[jnp.clip(starts[slot_e] + within, 0, t * TOP_K - 1)], 0)
    return dest, src.reshape(n_blk, 1, rows).astype(jnp.int32), blk_e, blk_valid, n_active


def moe_ffn(xa, row0_tiles, gain, mods, base_row, mod_idx, wr_pad, br_pad, w1, b1, w2, b2, layer,
            norm_final, final_norm):
    n_b = 8
    h, eidx, gate, rank, cnt = router(xa, row0_tiles, gain, mods, base_row + 3 * n_b,
                                      base_row + 4 * n_b, mod_idx, wr_pad, br_pad)
    tl = h.shape[0]
    eidx4 = eidx[:, :TOP_K]
    rank4 = rank[:, :TOP_K]
    counts = cnt[0, :N_EXPERTS].astype(jnp.int32)
    n_blk = (tl * TOP_K + N_EXPERTS * (EXPERT_ROWS - 1)) // EXPERT_ROWS
    dest, src, blk_e, blk_valid, n_active = _routing_tables(eidx4, rank4, counts, n_blk)
    xs = dispatch(h, src, n_active)
    ys = experts(xs, blk_e, blk_valid, n_active, w1, b1, w2, b2, layer)
    tmc = 256
    dest_km = dest.reshape(tl // tmc, tmc, TOP_K).transpose(0, 2, 1).reshape(tl // tmc, 1, TOP_K * tmc)
    return combine(ys, dest_km.astype(jnp.int32), xa, row0_tiles, gate, mods, base_row + 5 * n_b,
                   mod_idx, norm_final, final_norm)


def kernel(x, c, ctx, c_ctx, norm_mix, norm_ffn, w_ada, b_ada, w_in, conv_w, a_ws, a_bias, gla_wf, gla_bf, gla_wb, gla_bb, gla_norm, p_a, p_b, w_out, w_router, b_router, w1, b1, w2, b2, norm_final):
    n_b, seq, d = x.shape
    n_ctx = ctx.shape[1]
    depth = w_ada.shape[0]
    assert n_b * n_ctx == ROW_TILE and seq % ROW_TILE == 0 and n_b < 8
    tiles_per_batch = seq // ROW_TILE
    a_width = a_ws.shape[1] * a_ws.shape[2]
    dk = gla_wf.shape[2]
    dv = d

    def mod_idx(tile):
        return jnp.where(tile == 0, n_b, (tile - 1) // tiles_per_batch)

    xa = jnp.concatenate([ctx.reshape(n_b * n_ctx, d), x.reshape(n_b * seq, d)], axis=0)
    cond = jnp.zeros((8, d), F32).at[:n_b].set(c).at[n_b].set(c_ctx)
    mods_all = ada_mods(cond, w_ada, b_ada)
    mods = mods_all.reshape(depth, 8, N_MOD, d).transpose(0, 2, 1, 3).reshape(depth * N_MOD * 8, 1, d)

    c_u, c_q, c_r, c_lr, c_g = 0, 2 * a_width, 2 * a_width + 2 * dk + dv, 2 * a_width + 2 * dk + 2 * dv, \
        2 * a_width + 2 * dk + 2 * dv + 2 * GLA_RANK
    out = None
    for l in range(depth):
        last = l == depth - 1
        base = l * N_MOD * 8
        w_lr = jnp.zeros((1, d, LANES), F32).at[0, :, :2 * GLA_RANK].set(
            w_in[l, :, c_lr:c_lr + 2 * GLA_RANK])
        w_g = w_in[l:l + 1, :, c_g:]
        wf_pad = jnp.zeros((LANES, dk), F32).at[:GLA_RANK].set(gla_wf[l])
        wb_pad = jnp.zeros((LANES, dk), F32).at[GLA_RANK:2 * GLA_RANK].set(gla_wb[l])

        h = norm_mod(xa, norm_mix[l], mods, base + 0 * 8, base + 1 * 8, mod_idx)
        uv = matmul_act(h, w_in, l, c_u, 2 * a_width, act="gelu")
        qkv = matmul_act(h, w_in, l, c_q, 2 * dk + dv, act="conv", conv_w=conv_w[l],
                         ctx_seg=n_ctx, q_cols=dk, q_scale=float(dk // GLA_HEADS) ** -0.5)
        r_act = matmul_act(h, w_in, l, c_r, dv, act="silu")
        lr = matmul_act(h, w_lr, 0, 0, LANES, act="none", out_dtype=F32, tn=LANES)
        gates = matmul_act(h, w_g, 0, 0, 2 * d, act="sigmoid")

        n_ctx_blocks = n_b * n_ctx // GLA_BLOCK
        o_f = gla_scan(qkv, lr, None, None, wf_pad, gla_bf[l].reshape(1, dk), None,
                       n_b, n_ctx_blocks, reverse=False)
        y_gla = gla_scan(qkv, lr, r_act, o_f, wb_pad, gla_bb[l].reshape(1, dk),
                         gla_norm[l].reshape(1, -1), n_b, n_ctx_blocks, reverse=True)
        a = gmlp(uv, a_ws[l], a_bias[l].T)
        mrg = merge_proj(a, y_gla, gates, p_a, p_b, l)
        xa = out_proj(mrg, w_out, l, xa, mods, base + 2 * 8, mod_idx)

        wr_pad = jnp.zeros((d, LANES), F32).at[:, :N_EXPERTS].set(w_router[l])
        br_pad = jnp.full((1, LANES), -jnp.inf, F32).at[0, :N_EXPERTS].set(b_router[l])
        res = moe_ffn(xa, 1 if last else 0, norm_ffn[l], mods, base, mod_idx, wr_pad, br_pad,
                      w1, b1[l], w2, b2[l], l, norm_final, last)
        if last:
            out = res
        else:
            xa = res
    return out.reshape(n_b, seq, d)
```

```python
import functools

import jax
import jax.numpy as jnp
from jax import lax
from jax.experimental import pallas as pl
from jax.experimental.pallas import tpu as pltpu

F32 = jnp.float32
BF16 = jnp.bfloat16

GRID_W = 64
EPS = 1e-6
N_MOD = 6
A_GROUPS = 8
A_GROUP_W = 128
A_CHUNK = 128
GLA_HEADS = 4
GLA_RANK = 16
GLA_TAU = 16.0
GLA_CHUNK = 64
N_EXPERTS = 32
TOP_K = 4
SWIGLU_LIMIT = 7.0
SWIGLU_ALPHA = 1.702

LANES = 128
ROW_TILE = 1024
COL_TILE = 512
GLA_BLOCK = 256
GLA_HEADS_PER_STEP = 2
EXPERT_ROWS = 1024
EXPERT_SUB = 256
EXPERT_FT = 512
EXPERT_NT = 512
MIB = 1024 * 1024


def _cparams(vmem_mib, n_axes):
    return pltpu.CompilerParams(
        dimension_semantics=("arbitrary",) * n_axes,
        vmem_limit_bytes=int(vmem_mib * MIB))


def _silu(t):
    return t * jax.nn.sigmoid(t)


def _gelu_tanh(t):
    return 0.5 * t * (1.0 + jnp.tanh(0.7978845608028654 * (t + 0.044715 * (t * t * t))))


def _log_sigmoid(z):
    return jnp.minimum(z, 0.0) - jnp.log(1.0 + jnp.exp(-jnp.abs(z)))


def _ada_kernel(c_ref, w_ref, b_ref, o_ref):
    s = _silu(c_ref[...]).astype(BF16)
    w = w_ref[0].astype(BF16)
    o_ref[0] = jnp.dot(s, w, preferred_element_type=F32) + b_ref[0]


def ada_mods(cond, w_ada, b_ada):
    n_layer, d, n = w_ada.shape
    tn = 1024
    return pl.pallas_call(
        _ada_kernel,
        grid=(n_layer, n // tn),
        in_specs=[pl.BlockSpec((8, d), lambda l, j: (0, 0)),
                  pl.BlockSpec((1, d, tn), lambda l, j: (l, 0, j)),
                  pl.BlockSpec((1, 1, tn), lambda l, j: (l, 0, j))],
        out_specs=pl.BlockSpec((1, 8, tn), lambda l, j: (l, 0, j)),
        out_shape=jax.ShapeDtypeStruct((n_layer, 8, n), F32),
        compiler_params=_cparams(32, 2),
        name="ada_mods",
    )(cond, w_ada, b_ada.reshape(n_layer, 1, n))


def _norm_mod_kernel(x_ref, g_ref, sh_ref, sc_ref, o_ref):
    x = x_ref[...]
    r = lax.rsqrt(jnp.mean(x * x, axis=-1, keepdims=True) + EPS)
    h = (x * r) * g_ref[...]
    h = h * (1.0 + sc_ref[0]) + sh_ref[0]
    o_ref[...] = h.astype(o_ref.dtype)


def norm_mod(xa, gain, mods, sh_row, sc_row, mod_idx):
    t, d = xa.shape
    tm = 512
    per = ROW_TILE // tm
    return pl.pallas_call(
        _norm_mod_kernel,
        grid=(t // tm,),
        in_specs=[pl.BlockSpec((tm, d), lambda i: (i, 0)),
                  pl.BlockSpec((1, d), lambda i: (0, 0)),
                  pl.BlockSpec((1, 1, d), lambda i: (sh_row + mod_idx(i // per), 0, 0)),
                  pl.BlockSpec((1, 1, d), lambda i: (sc_row + mod_idx(i // per), 0, 0))],
        out_specs=pl.BlockSpec((tm, d), lambda i: (i, 0)),
        out_shape=jax.ShapeDtypeStruct((t, d), BF16),
        compiler_params=_cparams(32, 1),
        name="norm_mod",
    )(xa, gain.reshape(1, d), mods, mods)


def _mm_kernel(*refs, act, ctx_seg, q_tiles, q_scale):
    if act == "conv":
        x_ref, w_ref, cw_ref, o_ref, wb_ref = refs
    else:
        x_ref, w_ref, o_ref, wb_ref = refs
    j = pl.program_id(0)
    i = pl.program_id(1)

    @pl.when(i == 0)
    def _():
        wb_ref[...] = w_ref[...].astype(BF16)

    y = jnp.dot(x_ref[...], wb_ref[...], preferred_element_type=F32)
    if act == "gelu":
        y = _gelu_tanh(y)
    elif act == "silu":
        y = _silu(y)
    elif act == "sigmoid":
        y = jax.nn.sigmoid(y)
    elif act == "conv":
        seg = jnp.where(i == 0, ctx_seg, GRID_W)
        pos = lax.broadcasted_iota(jnp.int32, y.shape, 0) & (seg - 1)
        prev = jnp.where(pos == 0, 0.0, pltpu.roll(y, 1, 0))
        nxt = jnp.where(pos == seg - 1, 0.0, pltpu.roll(y, y.shape[0] - 1, 0))
        cw = cw_ref[...]
        y = _silu(cw[0:1] * prev + cw[1:2] * y + cw[2:3] * nxt)
        y = y * jnp.where(j < q_tiles, q_scale, 1.0)
    o_ref[...] = y.astype(o_ref.dtype)


def matmul_act(h, w, layer, col0, ncols, act="none", out_dtype=BF16, conv_w=None,
               ctx_seg=256, q_cols=0, q_scale=1.0, tn=COL_TILE):
    t, k = h.shape
    tm = ROW_TILE
    off = col0 // tn
    in_specs = [pl.BlockSpec((tm, k), lambda j, i: (i, 0)),
                pl.BlockSpec((None, k, tn), lambda j, i: (layer, 0, j + off))]
    args = [h, w]
    if act == "conv":
        in_specs.append(pl.BlockSpec((3, tn), lambda j, i: (0, j)))
        args.append(conv_w)
    return pl.pallas_call(
        functools.partial(_mm_kernel, act=act, ctx_seg=ctx_seg, q_tiles=q_cols // tn,
                          q_scale=q_scale),
        grid=(ncols // tn, t // tm),
        in_specs=in_specs,
        out_specs=pl.BlockSpec((tm, tn), lambda j, i: (i, j)),
        out_shape=jax.ShapeDtypeStruct((t, ncols), out_dtype),
        scratch_shapes=[pltpu.VMEM((k, tn), BF16)],
        compiler_params=_cparams(48, 2),
        name="matmul_" + act,
    )(*args)


def _gla_block(q, k, v, g, s_ref, tri, same, eye, reverse):
    c = GLA_CHUNK
    n_chunk = q.shape[0] // c
    cum = jnp.dot(tri, g, preferred_element_type=F32)
    tot = jnp.dot(same, g, preferred_element_type=F32)
    qt = (q * jnp.exp(cum)).astype(BF16)
    kt = (k * jnp.exp(-cum)).astype(BF16)
    ke = (k * jnp.exp(tot - cum)).astype(BF16)
    att = lax.dot_general(qt, kt, (((1,), (1,)), ((), ())), preferred_element_type=F32)
    att = jnp.where(tri > 0.0, att, 0.0).astype(BF16)
    o_intra = jnp.dot(att, v, preferred_element_type=F32)
    dk = g.shape[1]
    outs = [None] * n_chunk
    for ci in (range(n_chunk - 1, -1, -1) if reverse else range(n_chunk)):
        r0, r1 = ci * c, (ci + 1) * c
        s = s_ref[...]
        outs[ci] = o_intra[r0:r1] + jnp.dot(qt[r0:r1], s.astype(BF16), preferred_element_type=F32)
        tot_col = jnp.sum(jnp.where(eye, jnp.broadcast_to(tot[r0:r0 + 1], (dk, dk)), 0.0),
                          axis=1, keepdims=True)
        kv = lax.dot_general(ke[r0:r1], v[r0:r1], (((0,), (0,)), ((), ())),
                             preferred_element_type=F32)
        s_ref[...] = s * jnp.exp(tot_col) + kv
    return jnp.concatenate(outs, axis=0)


def _gla_kernel(*refs, reverse):
    if reverse:
        (q_ref, k_ref, v_ref, lr_ref, w_ref, b_ref, of_ref, r_ref, ng_ref, y_ref, s_ref) = refs
    else:
        (q_ref, k_ref, v_ref, lr_ref, w_ref, b_ref, o_ref, s_ref) = refs
    step = pl.program_id(2)

    @pl.when(step == 0)
    def _():
        s_ref[...] = jnp.zeros_like(s_ref)

    c = GLA_CHUNK
    n_head, dk, dv = s_ref.shape
    n_row = q_ref.shape[0]
    z = jnp.dot(lr_ref[...], w_ref[...], preferred_element_type=F32) + b_ref[...]
    g = _log_sigmoid(z) * (1.0 / GLA_TAU)
    row = lax.broadcasted_iota(jnp.int32, (n_row, n_row), 0)
    col = lax.broadcasted_iota(jnp.int32, (n_row, n_row), 1)
    shift = c.bit_length() - 1
    same_b = (row >> shift) == (col >> shift)
    same = same_b.astype(F32)
    tri = jnp.logical_and(same_b, (col >= row) if reverse else (col <= row)).astype(F32)
    eye = (lax.broadcasted_iota(jnp.int32, (dk, dk), 0)
           == lax.broadcasted_iota(jnp.int32, (dk, dk), 1))
    for hh in range(n_head):
        kc = pl.ds(hh * dk, dk)
        vc = pl.ds(hh * dv, dv)
        o = _gla_block(q_ref[:, kc].astype(F32), k_ref[:, kc].astype(F32), v_ref[:, vc],
                       g[:, hh * dk:(hh + 1) * dk], s_ref.at[hh], tri, same, eye, reverse)
        if reverse:
            o = o + of_ref[:, vc]
            o = o * lax.rsqrt(jnp.mean(o * o, axis=-1, keepdims=True) + EPS)
            y_ref[:, vc] = (o * ng_ref[...] * r_ref[:, vc].astype(F32)).astype(y_ref.dtype)
        else:
            o_ref[:, vc] = o


def gla_scan(qkv, lr, r_act, o_fwd, w_pad, bias, norm_g, n_batch, n_ctx_blocks, reverse):
    t = qkv.shape[0]
    dkh = w_pad.shape[1] // GLA_HEADS
    dv = qkv.shape[1] - 2 * w_pad.shape[1]
    dvh = dv // GLA_HEADS
    blk = GLA_BLOCK
    ctx_per_batch = n_ctx_blocks // n_batch
    assert ctx_per_batch == 1
    lat_per_batch = (t // blk - n_ctx_blocks) // n_batch
    n_step = ctx_per_batch + lat_per_batch

    def row_blk(b, s):
        if reverse:
            lat = n_ctx_blocks + lat_per_batch * b + (lat_per_batch - s)
        else:
            lat = n_ctx_blocks + lat_per_batch * b + (s - 1)
        return jnp.where(s == 0, b, lat)

    hp = GLA_HEADS_PER_STEP
    kw, vw = hp * dkh, hp * dvh
    k_off = w_pad.shape[1] // kw
    v_off = 2 * w_pad.shape[1] // vw
    in_specs = [pl.BlockSpec((blk, kw), lambda b, h, s: (row_blk(b, s), h)),
                pl.BlockSpec((blk, kw), lambda b, h, s: (row_blk(b, s), k_off + h)),
                pl.BlockSpec((blk, vw), lambda b, h, s: (row_blk(b, s), v_off + h)),
                pl.BlockSpec((blk, LANES), lambda b, h, s: (row_blk(b, s), 0)),
                pl.BlockSpec((LANES, kw), lambda b, h, s: (0, h)),
                pl.BlockSpec((1, kw), lambda b, h, s: (0, h))]
    args = [qkv, qkv, qkv, lr, w_pad, bias]
    if reverse:
        in_specs += [pl.BlockSpec((blk, vw), lambda b, h, s: (row_blk(b, s), h)),
                     pl.BlockSpec((blk, vw), lambda b, h, s: (row_blk(b, s), h)),
                     pl.BlockSpec((1, dvh), lambda b, h, s: (0, 0))]
        args += [o_fwd, r_act, norm_g]
        out_dtype = BF16
    else:
        out_dtype = F32
    return pl.pallas_call(
        functools.partial(_gla_kernel, reverse=reverse),
        grid=(n_batch, GLA_HEADS // hp, n_step),
        in_specs=in_specs,
        out_specs=pl.BlockSpec((blk, vw), lambda b, h, s: (row_blk(b, s), h)),
        out_shape=jax.ShapeDtypeStruct((t, dv), out_dtype),
        scratch_shapes=[pltpu.VMEM((hp, dkh, dvh), F32)],
        compiler_params=_cparams(32, 3),
        name="gla_bwd" if reverse else "gla_fwd",
    )(*args)


def _gmlp_kernel(u_ref, v_ref, ws_ref, bt_ref, o_ref):
    v = v_ref[...].astype(F32)
    mu = jnp.mean(v, axis=-1, keepdims=True)
    vc = v - mu
    var = jnp.mean(vc * vc, axis=-1, keepdims=True)
    vn = (vc * lax.rsqrt(var + EPS)).astype(BF16)
    n_chunk = v.shape[0] // A_CHUNK
    for gi in range(A_GROUPS):
        wsg = ws_ref[gi].astype(BF16)
        bias = bt_ref[:, gi:gi + 1]
        cols = slice(gi * A_GROUP_W, (gi + 1) * A_GROUP_W)
        for n in range(n_chunk):
            rows = slice(n * A_CHUNK, (n + 1) * A_CHUNK)
            s = jnp.dot(wsg, vn[rows, cols], preferred_element_type=F32) + bias
            o_ref[rows, cols] = (u_ref[rows, cols].astype(F32) * s).astype(o_ref.dtype)


def gmlp(uv, a_ws, a_bias_t):
    t = uv.shape[0]
    aw = uv.shape[1] // 2
    tm = 512
    return pl.pallas_call(
        _gmlp_kernel,
        grid=(t // tm,),
        in_specs=[pl.BlockSpec((tm, aw), lambda i: (i, 0)),
                  pl.BlockSpec((tm, aw), lambda i: (i, 1)),
                  pl.BlockSpec(a_ws.shape, lambda i: (0, 0, 0)),
                  pl.BlockSpec(a_bias_t.shape, lambda i: (0, 0))],
        out_specs=pl.BlockSpec((tm, aw), lambda i: (i, 0)),
        out_shape=jax.ShapeDtypeStruct((t, aw), BF16),
        compiler_params=_cparams(32, 1),
        name="gmlp",
    )(uv, uv, a_ws, a_bias_t)


def _merge_kernel(a_ref, y_ref, ga_ref, gb_ref, pa_ref, pb_ref, o_ref, pab_ref, pbb_ref):
    @pl.when(pl.program_id(1) == 0)
    def _():
        pab_ref[...] = pa_ref[...].astype(BF16)
        pbb_ref[...] = pb_ref[...].astype(BF16)

    ya = jnp.dot(a_ref[...], pab_ref[...], preferred_element_type=F32)
    yb = jnp.dot(y_ref[...], pbb_ref[...], preferred_element_type=F32)
    o_ref[...] = (ga_ref[...].astype(F32) * ya + gb_ref[...].astype(F32) * yb).astype(o_ref.dtype)


def merge_proj(a, yb, gates, p_a, p_b, layer):
    t, d = yb.shape
    tm, tn = ROW_TILE, COL_TILE
    nj = d // tn
    ka, kb = p_a.shape[1], p_b.shape[1]
    return pl.pallas_call(
        _merge_kernel,
        grid=(nj, t // tm),
        in_specs=[pl.BlockSpec((tm, ka), lambda j, i: (i, 0)),
                  pl.BlockSpec((tm, kb), lambda j, i: (i, 0)),
                  pl.BlockSpec((tm, tn), lambda j, i: (i, j)),
                  pl.BlockSpec((tm, tn), lambda j, i: (i, nj + j)),
                  pl.BlockSpec((None, ka, tn), lambda j, i: (layer, 0, j)),
                  pl.BlockSpec((None, kb, tn), lambda j, i: (layer, 0, j))],
        out_specs=pl.BlockSpec((tm, tn), lambda j, i: (i, j)),
        out_shape=jax.ShapeDtypeStruct((t, d), BF16),
        scratch_shapes=[pltpu.VMEM((ka, tn), BF16), pltpu.VMEM((kb, tn), BF16)],
        compiler_params=_cparams(48, 2),
        name="merge_proj",
    )(a, yb, gates, gates, p_a, p_b)


def _out_proj_kernel(m_ref, w_ref, x_ref, gt_ref, o_ref, wb_ref):
    @pl.when(pl.program_id(1) == 0)
    def _():
        wb_ref[...] = w_ref[...].astype(BF16)

    y = jnp.dot(m_ref[...], wb_ref[...], preferred_element_type=F32)
    o_ref[...] = x_ref[...] + gt_ref[0] * y


def out_proj(mrg, w_out, layer, xa, mods, gt_row, mod_idx):
    t, d = xa.shape
    tm, tn = ROW_TILE, COL_TILE
    return pl.pallas_call(
        _out_proj_kernel,
        grid=(d // tn, t // tm),
        in_specs=[pl.BlockSpec((tm, d), lambda j, i: (i, 0)),
                  pl.BlockSpec((None, d, tn), lambda j, i: (layer, 0, j)),
                  pl.BlockSpec((tm, tn), lambda j, i: (i, j)),
                  pl.BlockSpec((1, 1, tn), lambda j, i: (gt_row + mod_idx(i), 0, j))],
        out_specs=pl.BlockSpec((tm, tn), lambda j, i: (i, j)),
        out_shape=jax.ShapeDtypeStruct((t, d), F32),
        scratch_shapes=[pltpu.VMEM((d, tn), BF16)],
        compiler_params=_cparams(48, 2),
        name="out_proj",
    )(mrg, w_out, xa, mods)


def _router_kernel(x_ref, g_ref, sh_ref, sc_ref, wr_ref, br_ref,
                   h_ref, idx_ref, gate_ref, rank_ref, cnt_ref, carry_ref):
    @pl.when(pl.program_id(0) == 0)
    def _():
        carry_ref[...] = jnp.zeros_like(carry_ref)

    x = x_ref[...]
    r = lax.rsqrt(jnp.mean(x * x, axis=-1, keepdims=True) + EPS)
    h = (x * r) * g_ref[...]
    h = h * (1.0 + sc_ref[0]) + sh_ref[0]
    h_ref[...] = h
    logits = jnp.dot(h, wr_ref[...], preferred_element_type=F32) + br_ref[...]
    tm = x.shape[0]
    lane = lax.broadcasted_iota(jnp.int32, (tm, LANES), 1)
    lane_f = lane.astype(F32)
    member = jnp.zeros((tm, LANES), F32)
    vals, sels = [], []
    idx_out = jnp.zeros((tm, LANES), F32)
    for kk in range(TOP_K):
        m = jnp.max(logits, axis=1, keepdims=True)
        ik = jnp.min(jnp.where(logits == m, lane_f, float(LANES)), axis=1, keepdims=True)
        sel = lane_f == ik
        logits = jnp.where(sel, -jnp.inf, logits)
        member = member + sel.astype(F32)
        idx_out = jnp.where(lane == kk, ik, idx_out)
        vals.append(m)
        sels.append(sel)
    es = [jnp.exp(v - vals[0]) for v in vals]
    denom = es[0] + es[1] + es[2] + es[3]
    gate_out = jnp.zeros((tm, LANES), F32)
    for kk in range(TOP_K):
        gate_out = jnp.where(lane == kk, es[kk] / denom, gate_out)
    rr = lax.broadcasted_iota(jnp.int32, (tm, tm), 0)
    cc = lax.broadcasted_iota(jnp.int32, (tm, tm), 1)
    strict = (cc < rr).astype(BF16)
    before = jnp.dot(strict, member.astype(BF16), preferred_element_type=F32) + carry_ref[...]
    rank_out = jnp.zeros((tm, LANES), F32)
    for kk in range(TOP_K):
        rk = jnp.sum(jnp.where(sels[kk], before, 0.0), axis=1, keepdims=True)
        rank_out = jnp.where(lane == kk, rk, rank_out)
    carry_ref[...] = carry_ref[...] + jnp.sum(member, axis=0, keepdims=True)
    idx_ref[...] = idx_out.astype(jnp.int32)
    gate_ref[...] = gate_out
    rank_ref[...] = rank_out.astype(jnp.int32)
    cnt_ref[...] = carry_ref[...]


def router(xa, row0_tiles, gain, mods, sh_row, sc_row, mod_idx, wr_pad, br_pad):
    t, d = xa.shape
    tm = 512
    per = ROW_TILE // tm
    off = row0_tiles * per
    n = t // tm - off
    tl = n * tm
    small = lambda dt: jax.ShapeDtypeStruct((tl, LANES), dt)
    return pl.pallas_call(
        _router_kernel,
        grid=(n,),
        in_specs=[pl.BlockSpec((tm, d), lambda i: (i + off, 0)),
                  pl.BlockSpec((1, d), lambda i: (0, 0)),
                  pl.BlockSpec((1, 1, d), lambda i: (sh_row + mod_idx((i + off) // per), 0, 0)),
                  pl.BlockSpec((1, 1, d), lambda i: (sc_row + mod_idx((i + off) // per), 0, 0)),
                  pl.BlockSpec((d, LANES), lambda i: (0, 0)),
                  pl.BlockSpec((1, LANES), lambda i: (0, 0))],
        out_specs=[pl.BlockSpec((tm, d), lambda i: (i, 0)),
                   pl.BlockSpec((tm, LANES), lambda i: (i, 0)),
                   pl.BlockSpec((tm, LANES), lambda i: (i, 0)),
                   pl.BlockSpec((tm, LANES), lambda i: (i, 0)),
                   pl.BlockSpec((1, LANES), lambda i: (0, 0))],
        out_shape=[jax.ShapeDtypeStruct((tl, d), F32), small(jnp.int32), small(F32),
                   small(jnp.int32), jax.ShapeDtypeStruct((1, LANES), F32)],
        scratch_shapes=[pltpu.VMEM((1, LANES), F32)],
        compiler_params=_cparams(40, 1),
        name="router",
    )(xa, gain.reshape(1, d), mods, mods, wr_pad, br_pad)


DMA_UNROLL = 8


def _row_gather(idx_ref, n_rows, src_ref, dst_ref, sem):
    def body(g, carry):
        for u in range(DMA_UNROLL):
            r = g * DMA_UNROLL + u
            pltpu.make_async_copy(src_ref.at[pl.ds(idx_ref[0, 0, r], 1)],
                                  dst_ref.at[pl.ds(r, 1)], sem).start()
        return carry
    lax.fori_loop(0, n_rows // DMA_UNROLL, body, 0)


def _expert_kernel(be_ref, bv_ref, na_ref, cur_ref, nxt_ref, h_ref, w1g_ref, w1l_ref, b1g_ref,
                   b1l_ref, w2_ref, b2_ref, o_ref, gbuf, xs, act_s, sem, *, nf1):
    b = pl.program_id(0)
    j = pl.program_id(1)
    n_act = na_ref[0]
    active = b < n_act
    rows = EXPERT_ROWS
    n_sub = (bv_ref[b] + EXPERT_SUB - 1) // EXPERT_SUB
    jj = jnp.minimum(j, nf1 - 1)

    @pl.when(jnp.logical_and(active, j == 0))
    def _():
        @pl.when(b == 0)
        def _():
            _row_gather(cur_ref, rows, h_ref, gbuf, sem)

        pltpu.make_async_copy(h_ref.at[pl.ds(0, rows)], gbuf, sem).wait()
        xs[...] = gbuf[...].astype(BF16)

        @pl.when(b + 1 < n_act)
        def _():
            _row_gather(nxt_ref, rows, h_ref, gbuf, sem)

    @pl.when(jnp.logical_and(jnp.logical_not(active), j >= nf1))
    def _():
        o_ref[...] = jnp.zeros_like(o_ref)

    for m in range(1, rows // EXPERT_SUB + 1):
        r = m * EXPERT_SUB

        @pl.when(jnp.logical_and(active, jnp.logical_and(j < nf1, n_sub == m)))
        def _():
            x = xs[0:r, :]
            hg = jnp.dot(x, w1g_ref[0].astype(BF16), preferred_element_type=F32) + b1g_ref[0]
            hl = jnp.dot(x, w1l_ref[0].astype(BF16), preferred_element_type=F32) + b1l_ref[0]
            hg = jnp.minimum(hg, SWIGLU_LIMIT)
            hl = jnp.clip(hl, -SWIGLU_LIMIT, SWIGLU_LIMIT)
            act = hg * jax.nn.sigmoid(SWIGLU_ALPHA * hg) * (hl + 1.0)
            act_s[jj, 0:r, :] = act.astype(BF16)

        @pl.when(jnp.logical_and(active, jnp.logical_and(j >= nf1, n_sub == m)))
        def _():
            a = jnp.concatenate([act_s[t, 0:r, :] for t in range(nf1)], axis=1)
            y = jnp.dot(a, w2_ref[0].astype(BF16), preferred_element_type=F32) + b2_ref[0]
            o_ref[0:r, :] = y
            if r < rows:
                o_ref[r:rows, :] = jnp.zeros((rows - r, o_ref.shape[1]), o_ref.dtype)


def experts(h, src_tok, blk_e, blk_valid, n_active, w1, b1, w2, b2, layer):
    _, n_exp, d, f2 = w1.shape
    f = f2 // 2
    tf, tn = EXPERT_FT, EXPERT_NT
    nf1, nf2 = f // tf, d // tn
    n_blk = blk_e.shape[0]
    rows = EXPERT_ROWS

    def eb(b, be, na):
        return be[jnp.minimum(b, na[0] - 1)]

    def j1(b, j, na):
        return jnp.where(b < na[0], jnp.minimum(j, nf1 - 1), nf1 - 1)

    def j2(b, j, na):
        return jnp.where(b < na[0], jnp.maximum(j - nf1, 0), nf2 - 1)

    smem_rows = functools.partial(pl.BlockSpec, (1, 1, rows), memory_space=pltpu.SMEM)
    in_specs = [
        smem_rows(lambda b, j, be, bv, na: (b, 0, 0)),
        smem_rows(lambda b, j, be, bv, na: (jnp.minimum(b + 1, n_blk - 1), 0, 0)),
        pl.BlockSpec(memory_space=pl.ANY),
        pl.BlockSpec((None, 1, d, tf), lambda b, j, be, bv, na: (layer, eb(b, be, na), 0, j1(b, j, na))),
        pl.BlockSpec((None, 1, d, tf), lambda b, j, be, bv, na: (layer, eb(b, be, na), 0, nf1 + j1(b, j, na))),
        pl.BlockSpec((1, 1, tf), lambda b, j, be, bv, na: (eb(b, be, na), 0, j1(b, j, na))),
        pl.BlockSpec((1, 1, tf), lambda b, j, be, bv, na: (eb(b, be, na), 0, nf1 + j1(b, j, na))),
        pl.BlockSpec((None, 1, f, tn), lambda b, j, be, bv, na: (layer, eb(b, be, na), 0, j2(b, j, na))),
        pl.BlockSpec((1, 1, tn), lambda b, j, be, bv, na: (eb(b, be, na), 0, j2(b, j, na))),
    ]
    return pl.pallas_call(
        functools.partial(_expert_kernel, nf1=nf1),
        grid_spec=pltpu.PrefetchScalarGridSpec(
            num_scalar_prefetch=3,
            grid=(n_blk, nf1 + nf2),
            in_specs=in_specs,
            out_specs=pl.BlockSpec((rows, tn), lambda b, j, be, bv, na: (b, jnp.maximum(j - nf1, 0))),
            scratch_shapes=[pltpu.VMEM((rows, d), F32), pltpu.VMEM((rows, d), BF16),
                            pltpu.VMEM((nf1, rows, tf), BF16), pltpu.SemaphoreType.DMA(())]),
        out_shape=jax.ShapeDtypeStruct((n_blk * rows, d), F32),
        compiler_params=_cparams(58, 2),
        name="moe_experts",
    )(blk_e, blk_valid, n_active, src_tok, src_tok, h, w1, w1, b1.reshape(n_exp, 1, f2),
      b1.reshape(n_exp, 1, f2), w2, b2.reshape(n_exp, 1, d))


def _combine_kernel(dest_ref, y_ref, x_ref, gate_ref, gt_ref, ng_ref, o_ref, buf_ref, sem,
                    *, final_norm):
    tm = x_ref.shape[0]
    _row_gather(dest_ref, tm * TOP_K, y_ref, buf_ref, sem)
    pltpu.make_async_copy(y_ref.at[pl.ds(0, tm * TOP_K)], buf_ref, sem).wait()
    gate = gate_ref[...]
    acc = jnp.zeros(x_ref.shape, F32)
    for kk in range(TOP_K):
        acc = acc + gate[:, kk:kk + 1] * buf_ref[pl.ds(kk * tm, tm), :]
    xn = x_ref[...] + gt_ref[0] * acc
    if final_norm:
        xn = xn * lax.rsqrt(jnp.mean(xn * xn, axis=-1, keepdims=True) + EPS) * ng_ref[...]
    o_ref[...] = xn


def combine(y_sorted, dest_km, xa, row0_tiles, gate, mods, gt_row, mod_idx, norm_g, final_norm):
    t, d = xa.shape
    tm = 256
    per = ROW_TILE // tm
    off = row0_tiles * per
    n = t // tm - off
    out_rows = n * tm if final_norm else t
    out_off = 0 if final_norm else off
    kernel = functools.partial(_combine_kernel, final_norm=final_norm)
    call = pl.pallas_call(
        kernel,
        grid=(n,),
        in_specs=[pl.BlockSpec((1, 1, TOP_K * tm), lambda i: (i, 0, 0), memory_space=pltpu.SMEM),
                  pl.BlockSpec(memory_space=pl.ANY),
                  pl.BlockSpec((tm, d), lambda i: (i + off, 0)),
                  pl.BlockSpec((tm, LANES), lambda i: (i, 0)),
                  pl.BlockSpec((1, 1, d), lambda i: (gt_row + mod_idx((i + off) // per), 0, 0)),
                  pl.BlockSpec((1, d), lambda i: (0, 0))],
        out_specs=pl.BlockSpec((tm, d), lambda i: (i + out_off, 0)),
        out_shape=jax.ShapeDtypeStruct((out_rows, d), F32),
        scratch_shapes=[pltpu.VMEM((TOP_K * tm, d), F32), pltpu.SemaphoreType.DMA(())],
        compiler_params=_cparams(32, 1),
        name="moe_combine",
    )
    return call(dest_km, y_sorted, xa, gate, mods, norm_g.reshape(1, d))


def _routing_tables(eidx, rank, counts, n_blk):
    t = eidx.shape[0]
    rows = EXPERT_ROWS
    padded = (counts + rows - 1) // rows * rows
    p_ends = jnp.cumsum(padded)
    p_starts = p_ends - padded
    starts = jnp.cumsum(counts) - counts
    dest = p_starts[eidx] + rank
    n_active = (p_ends[-1] // rows).astype(jnp.int32).reshape(1)
    blk_start = jnp.arange(n_blk, dtype=jnp.int32) * rows
    blk_e = jnp.minimum(jnp.searchsorted(p_ends, blk_start, side="right"), N_EXPERTS - 1).astype(jnp.int32)
    blk_valid = jnp.clip(counts[blk_e] - (blk_start - p_starts[blk_e]), 0, rows).astype(jnp.int32)
    order = jnp.argsort(eidx.reshape(-1), stable=True).astype(jnp.int32)
    tok_sorted = order // TOP_K
    slot = jnp.arange(n_blk * rows, dtype=jnp.int32)
    slot_e = jnp.repeat(blk_e, rows)
    within = slot - p_starts[slot_e]
    src = jnp.where(within < counts[slot_e],
                    tok_sorted[jnp.clip(starts[slot_e] + within, 0, t * TOP_K - 1)], 0)
    return dest, src.reshape(n_blk, 1, rows).astype(jnp.int32), blk_e, blk_valid, n_active


def moe_ffn(xa, row0_tiles, gain, mods, base_row, mod_idx, wr_pad, br_pad, w1, b1, w2, b2, layer,
            norm_final, final_norm):
    n_b = 8
    h, eidx, gate, rank, cnt = router(xa, row0_tiles, gain, mods, base_row + 3 * n_b,
                                      base_row + 4 * n_b, mod_idx, wr_pad, br_pad)
    tl = h.shape[0]
    eidx4 = eidx[:, :TOP_K]
    rank4 = rank[:, :TOP_K]
    counts = cnt[0, :N_EXPERTS].astype(jnp.int32)
    n_blk = (tl * TOP_K + N_EXPERTS * (EXPERT_ROWS - 1)) // EXPERT_ROWS
    dest, src, blk_e, blk_valid, n_active = _routing_tables(eidx4, rank4, counts, n_blk)
    ys = experts(h, src, blk_e, blk_valid, n_active, w1, b1, w2, b2, layer)
    tmc = 256
    dest_km = dest.reshape(tl // tmc, tmc, TOP_K).transpose(0, 2, 1).reshape(tl // tmc, 1, TOP_K * tmc)
    return combine(ys, dest_km.astype(jnp.int32), xa, row0_tiles, gate, mods, base_row + 5 * n_b,
                   mod_idx, norm_final, final_norm)


def kernel(x, c, ctx, c_ctx, norm_mix, norm_ffn, w_ada, b_ada, w_in, conv_w, a_ws, a_bias, gla_wf, gla_bf, gla_wb, gla_bb, gla_norm, p_a, p_b, w_out, w_router, b_router, w1, b1, w2, b2, norm_final):
    n_b, seq, d = x.shape
    n_ctx = ctx.shape[1]
    depth = w_ada.shape[0]
    assert n_b * n_ctx == ROW_TILE and seq % ROW_TILE == 0 and n_b < 8
    tiles_per_batch = seq // ROW_TILE
    a_width = a_ws.shape[1] * a_ws.shape[2]
    dk = gla_wf.shape[2]
    dv = d

    def mod_idx(tile):
        return jnp.where(tile == 0, n_b, (tile - 1) // tiles_per_batch)

    xa = jnp.concatenate([ctx.reshape(n_b * n_ctx, d), x.reshape(n_b * seq, d)], axis=0)
    cond = jnp.zeros((8, d), F32).at[:n_b].set(c).at[n_b].set(c_ctx)
    mods_all = ada_mods(cond, w_ada, b_ada)
    mods = mods_all.reshape(depth, 8, N_MOD, d).transpose(0, 2, 1, 3).reshape(depth * N_MOD * 8, 1, d)

    c_u, c_q, c_r, c_lr, c_g = 0, 2 * a_width, 2 * a_width + 2 * dk + dv, 2 * a_width + 2 * dk + 2 * dv, \
        2 * a_width + 2 * dk + 2 * dv + 2 * GLA_RANK
    out = None
    for l in range(depth):
        last = l == depth - 1
        base = l * N_MOD * 8
        w_lr = jnp.zeros((1, d, LANES), F32).at[0, :, :2 * GLA_RANK].set(
            w_in[l, :, c_lr:c_lr + 2 * GLA_RANK])
        w_g = w_in[l:l + 1, :, c_g:]
        wf_pad = jnp.zeros((LANES, dk), F32).at[:GLA_RANK].set(gla_wf[l])
        wb_pad = jnp.zeros((LANES, dk), F32).at[GLA_RANK:2 * GLA_RANK].set(gla_wb[l])

        h = norm_mod(xa, norm_mix[l], mods, base + 0 * 8, base + 1 * 8, mod_idx)
        uv = matmul_act(h, w_in, l, c_u, 2 * a_width, act="gelu")
        qkv = matmul_act(h, w_in, l, c_q, 2 * dk + dv, act="conv", conv_w=conv_w[l],
                         ctx_seg=n_ctx, q_cols=dk, q_scale=float(dk // GLA_HEADS) ** -0.5)
        r_act = matmul_act(h, w_in, l, c_r, dv, act="silu")
        lr = matmul_act(h, w_lr, 0, 0, LANES, act="none", out_dtype=F32, tn=LANES)
        gates = matmul_act(h, w_g, 0, 0, 2 * d, act="sigmoid")

        n_ctx_blocks = n_b * n_ctx // GLA_BLOCK
        o_f = gla_scan(qkv, lr, None, None, wf_pad, gla_bf[l].reshape(1, dk), None,
                       n_b, n_ctx_blocks, reverse=False)
        y_gla = gla_scan(qkv, lr, r_act, o_f, wb_pad, gla_bb[l].reshape(1, dk),
                         gla_norm[l].reshape(1, -1), n_b, n_ctx_blocks, reverse=True)
        a = gmlp(uv, a_ws[l], a_bias[l].T)
        mrg = merge_proj(a, y_gla, gates, p_a, p_b, l)
        xa = out_proj(mrg, w_out, l, xa, mods, base + 2 * 8, mod_idx)

        wr_pad = jnp.zeros((d, LANES), F32).at[:, :N_EXPERTS].set(w_router[l])
        br_pad = jnp.full((1, LANES), -jnp.inf, F32).at[0, :N_EXPERTS].set(b_router[l])
        res = moe_ffn(xa, 1 if last else 0, norm_ffn[l], mods, base, mod_idx, wr_pad, br_pad,
                      w1, b1[l], w2, b2[l], l, norm_final, last)
        if last:
            out = res
        else:
            xa = res
    return out.reshape(n_b, seq, d)
```

```python
import functools

import jax
import jax.numpy as jnp
from jax import lax
from jax.experimental import pallas as pl
from jax.experimental.pallas import tpu as pltpu

F32 = jnp.float32
BF16 = jnp.bfloat16

GRID_W = 64
EPS = 1e-6
N_MOD = 6
A_GROUPS = 8
A_GROUP_W = 128
A_CHUNK = 128
GLA_HEADS = 4
GLA_RANK = 16
GLA_TAU = 16.0
GLA_CHUNK = 64
N_EXPERTS = 32
TOP_K = 4
SWIGLU_LIMIT = 7.0
SWIGLU_ALPHA = 1.702

LANES = 128
ROW_TILE = 1024
COL_TILE = 512
GLA_BLOCK = 256
GLA_HEADS_PER_STEP = 2
EXPERT_ROWS = 1024
EXPERT_SUB = 256
EXPERT_FT = 512
EXPERT_NT = 512
MIB = 1024 * 1024


def _cparams(vmem_mib, n_axes):
    return pltpu.CompilerParams(
        dimension_semantics=("arbitrary",) * n_axes,
        vmem_limit_bytes=int(vmem_mib * MIB))


def _sigmoid(t):
    return 0.5 * (1.0 + jnp.tanh(0.5 * t))


def _silu(t):
    return t * _sigmoid(t)


def _gelu_tanh(t):
    return 0.5 * t * (1.0 + jnp.tanh(0.7978845608028654 * (t + 0.044715 * (t * t * t))))


def _log_sigmoid(z):
    return jnp.minimum(z, 0.0) - jnp.log(1.0 + jnp.exp(-jnp.abs(z)))


def _ada_kernel(c_ref, w_ref, b_ref, o_ref):
    s = _silu(c_ref[...]).astype(BF16)
    w = w_ref[0].astype(BF16)
    o_ref[0] = jnp.dot(s, w, preferred_element_type=F32) + b_ref[0]


def ada_mods(cond, w_ada, b_ada):
    n_layer, d, n = w_ada.shape
    tn = 1024
    return pl.pallas_call(
        _ada_kernel,
        grid=(n_layer, n // tn),
        in_specs=[pl.BlockSpec((8, d), lambda l, j: (0, 0)),
                  pl.BlockSpec((1, d, tn), lambda l, j: (l, 0, j)),
                  pl.BlockSpec((1, 1, tn), lambda l, j: (l, 0, j))],
        out_specs=pl.BlockSpec((1, 8, tn), lambda l, j: (l, 0, j)),
        out_shape=jax.ShapeDtypeStruct((n_layer, 8, n), F32),
        compiler_params=_cparams(32, 2),
        name="ada_mods",
    )(cond, w_ada, b_ada.reshape(n_layer, 1, n))


def _stream_specs(stream, tm, tn, row_of, col_of):
    if not isinstance(stream, tuple):
        return [pl.BlockSpec((tm, tn), lambda *g: (row_of(*g), col_of(*g)))], [stream], 0
    ctx2d, lat2d = stream
    nct = ctx2d.shape[0] // tm
    specs = [pl.BlockSpec((tm, tn), lambda *g: (jnp.minimum(row_of(*g), nct - 1), col_of(*g))),
             pl.BlockSpec((tm, tn), lambda *g: (jnp.maximum(row_of(*g) - nct, 0), col_of(*g)))]
    return specs, [ctx2d, lat2d], nct


def _stream_tile(refs, tile, n_ctx_tiles):
    if len(refs) == 1:
        return refs[0][...]
    return jnp.where(tile < n_ctx_tiles, refs[0][...], refs[1][...])


def _norm_mod_kernel(*refs, n_ctx_tiles):
    g_ref, sh_ref, sc_ref, o_ref = refs[-4:]
    x = _stream_tile(refs[:-4], pl.program_id(0), n_ctx_tiles)
    r = lax.rsqrt(jnp.mean(x * x, axis=-1, keepdims=True) + EPS)
    h = (x * r) * g_ref[...]
    h = h * (1.0 + sc_ref[0]) + sh_ref[0]
    o_ref[...] = h.astype(o_ref.dtype)


def norm_mod(stream, gain, mods, sh_row, sc_row, mod_idx):
    t = sum(s.shape[0] for s in stream) if isinstance(stream, tuple) else stream.shape[0]
    d = gain.shape[0]
    tm = 512
    per = ROW_TILE // tm
    specs, arrays, nct = _stream_specs(stream, tm, d, lambda i: i, lambda i: 0)
    return pl.pallas_call(
        functools.partial(_norm_mod_kernel, n_ctx_tiles=nct),
        grid=(t // tm,),
        in_specs=specs + [
            pl.BlockSpec((1, d), lambda i: (0, 0)),
            pl.BlockSpec((1, 1, d), lambda i: (sh_row + mod_idx(i // per), 0, 0)),
            pl.BlockSpec((1, 1, d), lambda i: (sc_row + mod_idx(i // per), 0, 0))],
        out_specs=pl.BlockSpec((tm, d), lambda i: (i, 0)),
        out_shape=jax.ShapeDtypeStruct((t, d), BF16),
        compiler_params=_cparams(40, 1),
        name="norm_mod",
    )(*arrays, gain.reshape(1, d), mods, mods)


def _mm_kernel(*refs, act, ctx_seg, q_tiles, q_scale):
    if act == "conv":
        x_ref, w_ref, cw_ref, o_ref, wb_ref = refs
    else:
        x_ref, w_ref, o_ref, wb_ref = refs
    j = pl.program_id(0)
    i = pl.program_id(1)

    @pl.when(i == 0)
    def _():
        wb_ref[...] = w_ref[...].astype(BF16)

    y = jnp.dot(x_ref[...], wb_ref[...], preferred_element_type=F32)
    if act == "gelu":
        y = _gelu_tanh(y)
    elif act == "silu":
        y = _silu(y.astype(o_ref.dtype))
    elif act == "sigmoid":
        y = _sigmoid(y.astype(o_ref.dtype))
    elif act == "conv":
        seg = jnp.where(i == 0, ctx_seg, GRID_W)
        pos = lax.broadcasted_iota(jnp.int32, y.shape, 0) & (seg - 1)
        prev = jnp.where(pos == 0, 0.0, pltpu.roll(y, 1, 0))
        nxt = jnp.where(pos == seg - 1, 0.0, pltpu.roll(y, y.shape[0] - 1, 0))
        cw = cw_ref[...]
        y = _silu((cw[0:1] * prev + cw[1:2] * y + cw[2:3] * nxt).astype(o_ref.dtype))
        y = y * jnp.where(j < q_tiles, q_scale, 1.0).astype(o_ref.dtype)
    o_ref[...] = y.astype(o_ref.dtype)


def matmul_act(h, w, layer, col0, ncols, act="none", out_dtype=BF16, conv_w=None,
               ctx_seg=256, q_cols=0, q_scale=1.0, tn=COL_TILE):
    t, k = h.shape
    tm = ROW_TILE
    off = col0 // tn
    in_specs = [pl.BlockSpec((tm, k), lambda j, i: (i, 0)),
                pl.BlockSpec((None, k, tn), lambda j, i: (layer, 0, j + off))]
    args = [h, w]
    if act == "conv":
        in_specs.append(pl.BlockSpec((3, tn), lambda j, i: (0, j)))
        args.append(conv_w)
    return pl.pallas_call(
        functools.partial(_mm_kernel, act=act, ctx_seg=ctx_seg, q_tiles=q_cols // tn,
                          q_scale=q_scale),
        grid=(ncols // tn, t // tm),
        in_specs=in_specs,
        out_specs=pl.BlockSpec((tm, tn), lambda j, i: (i, j)),
        out_shape=jax.ShapeDtypeStruct((t, ncols), out_dtype),
        scratch_shapes=[pltpu.VMEM((k, tn), BF16)],
        compiler_params=_cparams(48, 2),
        name="matmul_" + act,
    )(*args)


def _gla_block(q, k, v, g, s_ref, tri, same, eye, reverse):
    c = GLA_CHUNK
    n_chunk = q.shape[0] // c
    cum = jnp.dot(tri, g, preferred_element_type=F32)
    tot = jnp.dot(same, g, preferred_element_type=F32)
    qt = (q * jnp.exp(cum)).astype(BF16)
    kt = (k * jnp.exp(-cum)).astype(BF16)
    ke = (k * jnp.exp(tot - cum)).astype(BF16)
    att = lax.dot_general(qt, kt, (((1,), (1,)), ((), ())), preferred_element_type=F32)
    att = jnp.where(tri > 0.0, att, 0.0).astype(BF16)
    o_intra = jnp.dot(att, v, preferred_element_type=F32)
    dk = g.shape[1]
    outs = [None] * n_chunk
    for ci in (range(n_chunk - 1, -1, -1) if reverse else range(n_chunk)):
        r0, r1 = ci * c, (ci + 1) * c
        s = s_ref[...]
        outs[ci] = o_intra[r0:r1] + jnp.dot(qt[r0:r1], s.astype(BF16), preferred_element_type=F32)
        tot_col = jnp.sum(jnp.where(eye, jnp.broadcast_to(tot[r0:r0 + 1], (dk, dk)), 0.0),
                          axis=1, keepdims=True)
        kv = lax.dot_general(ke[r0:r1], v[r0:r1], (((0,), (0,)), ((), ())),
                             preferred_element_type=F32)
        s_ref[...] = s * jnp.exp(tot_col) + kv
    return jnp.concatenate(outs, axis=0)


def _gla_kernel(*refs, reverse):
    if reverse:
        (q_ref, k_ref, v_ref, lr_ref, w_ref, b_ref, of_ref, r_ref, ng_ref, y_ref, s_ref) = refs
    else:
        (q_ref, k_ref, v_ref, lr_ref, w_ref, b_ref, o_ref, s_ref) = refs
    step = pl.program_id(2)

    @pl.when(step == 0)
    def _():
        s_ref[...] = jnp.zeros_like(s_ref)

    c = GLA_CHUNK
    n_head, dk, dv = s_ref.shape
    n_row = q_ref.shape[0]
    z = jnp.dot(lr_ref[...], w_ref[...], preferred_element_type=F32) + b_ref[...]
    g = _log_sigmoid(z) * (1.0 / GLA_TAU)
    row = lax.broadcasted_iota(jnp.int32, (n_row, n_row), 0)
    col = lax.broadcasted_iota(jnp.int32, (n_row, n_row), 1)
    shift = c.bit_length() - 1
    same_b = (row >> shift) == (col >> shift)
    same = same_b.astype(F32)
    tri = jnp.logical_and(same_b, (col >= row) if reverse else (col <= row)).astype(F32)
    eye = (lax.broadcasted_iota(jnp.int32, (dk, dk), 0)
           == lax.broadcasted_iota(jnp.int32, (dk, dk), 1))
    for hh in range(n_head):
        kc = pl.ds(hh * dk, dk)
        vc = pl.ds(hh * dv, dv)
        o = _gla_block(q_ref[:, kc].astype(F32), k_ref[:, kc].astype(F32), v_ref[:, vc],
                       g[:, hh * dk:(hh + 1) * dk], s_ref.at[hh], tri, same, eye, reverse)
        if reverse:
            o = o + of_ref[:, vc]
            o = o * lax.rsqrt(jnp.mean(o * o, axis=-1, keepdims=True) + EPS)
            y_ref[:, vc] = (o * ng_ref[...] * r_ref[:, vc].astype(F32)).astype(y_ref.dtype)
        else:
            o_ref[:, vc] = o


def gla_scan(qkv, lr, r_act, o_fwd, w_pad, bias, norm_g, n_batch, n_ctx_blocks, reverse):
    t = qkv.shape[0]
    dkh = w_pad.shape[1] // GLA_HEADS
    dv = qkv.shape[1] - 2 * w_pad.shape[1]
    dvh = dv // GLA_HEADS
    blk = GLA_BLOCK
    ctx_per_batch = n_ctx_blocks // n_batch
    assert ctx_per_batch == 1
    lat_per_batch = (t // blk - n_ctx_blocks) // n_batch
    n_step = ctx_per_batch + lat_per_batch

    def row_blk(b, s):
        if reverse:
            lat = n_ctx_blocks + lat_per_batch * b + (lat_per_batch - s)
        else:
            lat = n_ctx_blocks + lat_per_batch * b + (s - 1)
        return jnp.where(s == 0, b, lat)

    hp = GLA_HEADS_PER_STEP
    kw, vw = hp * dkh, hp * dvh
    k_off = w_pad.shape[1] // kw
    v_off = 2 * w_pad.shape[1] // vw
    in_specs = [pl.BlockSpec((blk, kw), lambda b, h, s: (row_blk(b, s), h)),
                pl.BlockSpec((blk, kw), lambda b, h, s: (row_blk(b, s), k_off + h)),
                pl.BlockSpec((blk, vw), lambda b, h, s: (row_blk(b, s), v_off + h)),
                pl.BlockSpec((blk, LANES), lambda b, h, s: (row_blk(b, s), 0)),
                pl.BlockSpec((LANES, kw), lambda b, h, s: (0, h)),
                pl.BlockSpec((1, kw), lambda b, h, s: (0, h))]
    args = [qkv, qkv, qkv, lr, w_pad, bias]
    if reverse:
        in_specs += [pl.BlockSpec((blk, vw), lambda b, h, s: (row_blk(b, s), h)),
                     pl.BlockSpec((blk, vw), lambda b, h, s: (row_blk(b, s), h)),
                     pl.BlockSpec((1, dvh), lambda b, h, s: (0, 0))]
        args += [o_fwd, r_act, norm_g]
        out_dtype = BF16
    else:
        out_dtype = F32
    return pl.pallas_call(
        functools.partial(_gla_kernel, reverse=reverse),
        grid=(n_batch, GLA_HEADS // hp, n_step),
        in_specs=in_specs,
        out_specs=pl.BlockSpec((blk, vw), lambda b, h, s: (row_blk(b, s), h)),
        out_shape=jax.ShapeDtypeStruct((t, dv), out_dtype),
        scratch_shapes=[pltpu.VMEM((hp, dkh, dvh), F32)],
        compiler_params=_cparams(32, 3),
        name="gla_bwd" if reverse else "gla_fwd",
    )(*args)


def _gmlp_kernel(u_ref, v_ref, ws_ref, bt_ref, o_ref):
    v = v_ref[...].astype(F32)
    mu = jnp.mean(v, axis=-1, keepdims=True)
    vc = v - mu
    var = jnp.mean(vc * vc, axis=-1, keepdims=True)
    vn = (vc * lax.rsqrt(var + EPS)).astype(BF16)
    n_chunk = v.shape[0] // A_CHUNK
    for gi in range(A_GROUPS):
        wsg = ws_ref[gi].astype(BF16)
        bias = bt_ref[:, gi:gi + 1]
        cols = slice(gi * A_GROUP_W, (gi + 1) * A_GROUP_W)
        for n in range(n_chunk):
            rows = slice(n * A_CHUNK, (n + 1) * A_CHUNK)
            s = jnp.dot(wsg, vn[rows, cols], preferred_element_type=F32) + bias
            o_ref[rows, cols] = (u_ref[rows, cols].astype(F32) * s).astype(o_ref.dtype)


def gmlp(uv, a_ws, a_bias_t):
    t = uv.shape[0]
    aw = uv.shape[1] // 2
    tm = 512
    return pl.pallas_call(
        _gmlp_kernel,
        grid=(t // tm,),
        in_specs=[pl.BlockSpec((tm, aw), lambda i: (i, 0)),
                  pl.BlockSpec((tm, aw), lambda i: (i, 1)),
                  pl.BlockSpec(a_ws.shape, lambda i: (0, 0, 0)),
                  pl.BlockSpec(a_bias_t.shape, lambda i: (0, 0))],
        out_specs=pl.BlockSpec((tm, aw), lambda i: (i, 0)),
        out_shape=jax.ShapeDtypeStruct((t, aw), BF16),
        compiler_params=_cparams(32, 1),
        name="gmlp",
    )(uv, uv, a_ws, a_bias_t)


def _merge_kernel(a_ref, y_ref, ga_ref, gb_ref, pa_ref, pb_ref, o_ref, pab_ref, pbb_ref):
    @pl.when(pl.program_id(1) == 0)
    def _():
        pab_ref[...] = pa_ref[...].astype(BF16)
        pbb_ref[...] = pb_ref[...].astype(BF16)

    ya = jnp.dot(a_ref[...], pab_ref[...], preferred_element_type=F32)
    yb = jnp.dot(y_ref[...], pbb_ref[...], preferred_element_type=F32)
    o_ref[...] = (ga_ref[...].astype(F32) * ya + gb_ref[...].astype(F32) * yb).astype(o_ref.dtype)


def merge_proj(a, yb, gates, p_a, p_b, layer):
    t, d = yb.shape
    tm, tn = ROW_TILE, COL_TILE
    nj = d // tn
    ka, kb = p_a.shape[1], p_b.shape[1]
    return pl.pallas_call(
        _merge_kernel,
        grid=(nj, t // tm),
        in_specs=[pl.BlockSpec((tm, ka), lambda j, i: (i, 0)),
                  pl.BlockSpec((tm, kb), lambda j, i: (i, 0)),
                  pl.BlockSpec((tm, tn), lambda j, i: (i, j)),
                  pl.BlockSpec((tm, tn), lambda j, i: (i, nj + j)),
                  pl.BlockSpec((None, ka, tn), lambda j, i: (layer, 0, j)),
                  pl.BlockSpec((None, kb, tn), lambda j, i: (layer, 0, j))],
        out_specs=pl.BlockSpec((tm, tn), lambda j, i: (i, j)),
        out_shape=jax.ShapeDtypeStruct((t, d), BF16),
        scratch_shapes=[pltpu.VMEM((ka, tn), BF16), pltpu.VMEM((kb, tn), BF16)],
        compiler_params=_cparams(48, 2),
        name="merge_proj",
    )(a, yb, gates, gates, p_a, p_b)


def _out_proj_kernel(*refs, n_ctx_tiles):
    m_ref, w_ref = refs[:2]
    gt_ref, o_ref, wb_ref = refs[-3:]
    i = pl.program_id(1)

    @pl.when(i == 0)
    def _():
        wb_ref[...] = w_ref[...].astype(BF16)

    y = jnp.dot(m_ref[...], wb_ref[...], preferred_element_type=F32)
    o_ref[...] = _stream_tile(refs[2:-3], i, n_ctx_tiles) + gt_ref[0] * y


def out_proj(mrg, w_out, layer, stream, mods, gt_row, mod_idx):
    t, d = mrg.shape
    tm, tn = ROW_TILE, COL_TILE
    specs, arrays, nct = _stream_specs(stream, tm, tn, lambda j, i: i, lambda j, i: j)
    return pl.pallas_call(
        functools.partial(_out_proj_kernel, n_ctx_tiles=nct),
        grid=(d // tn, t // tm),
        in_specs=[pl.BlockSpec((tm, d), lambda j, i: (i, 0)),
                  pl.BlockSpec((None, d, tn), lambda j, i: (layer, 0, j))] + specs + [
                  pl.BlockSpec((1, 1, tn), lambda j, i: (gt_row + mod_idx(i), 0, j))],
        out_specs=pl.BlockSpec((tm, tn), lambda j, i: (i, j)),
        out_shape=jax.ShapeDtypeStruct((t, d), F32),
        scratch_shapes=[pltpu.VMEM((d, tn), BF16)],
        compiler_params=_cparams(48, 2),
        name="out_proj",
    )(mrg, w_out, *arrays, mods)


def _router_kernel(x_ref, g_ref, sh_ref, sc_ref, wr_ref, br_ref,
                   h_ref, idx_ref, gate_ref, rank_ref, cnt_ref, carry_ref):
    @pl.when(pl.program_id(0) == 0)
    def _():
        carry_ref[...] = jnp.zeros_like(carry_ref)

    x = x_ref[...]
    r = lax.rsqrt(jnp.mean(x * x, axis=-1, keepdims=True) + EPS)
    h = (x * r) * g_ref[...]
    h = h * (1.0 + sc_ref[0]) + sh_ref[0]
    h_ref[...] = h
    logits = jnp.dot(h, wr_ref[...], preferred_element_type=F32) + br_ref[...]
    tm = x.shape[0]
    lane = lax.broadcasted_iota(jnp.int32, (tm, LANES), 1)
    lane_f = lane.astype(F32)
    member = jnp.zeros((tm, LANES), F32)
    vals, sels = [], []
    idx_out = jnp.zeros((tm, LANES), F32)
    for kk in range(TOP_K):
        m = jnp.max(logits, axis=1, keepdims=True)
        ik = jnp.min(jnp.where(logits == m, lane_f, float(LANES)), axis=1, keepdims=True)
        sel = lane_f == ik
        logits = jnp.where(sel, -jnp.inf, logits)
        member = member + sel.astype(F32)
        idx_out = jnp.where(lane == kk, ik, idx_out)
        vals.append(m)
        sels.append(sel)
    es = [jnp.exp(v - vals[0]) for v in vals]
    denom = es[0] + es[1] + es[2] + es[3]
    gate_out = jnp.zeros((tm, LANES), F32)
    for kk in range(TOP_K):
        gate_out = jnp.where(lane == kk, es[kk] / denom, gate_out)
    rr = lax.broadcasted_iota(jnp.int32, (tm, tm), 0)
    cc = lax.broadcasted_iota(jnp.int32, (tm, tm), 1)
    strict = (cc < rr).astype(BF16)
    before = jnp.dot(strict, member.astype(BF16), preferred_element_type=F32) + carry_ref[...]
    rank_out = jnp.zeros((tm, LANES), F32)
    for kk in range(TOP_K):
        rk = jnp.sum(jnp.where(sels[kk], before, 0.0), axis=1, keepdims=True)
        rank_out = jnp.where(lane == kk, rk, rank_out)
    carry_ref[...] = carry_ref[...] + jnp.sum(member, axis=0, keepdims=True)
    idx_ref[...] = idx_out.astype(jnp.int32)
    gate_ref[...] = gate_out
    rank_ref[...] = rank_out.astype(jnp.int32)
    cnt_ref[...] = carry_ref[...]


def router(xa, row0_tiles, gain, mods, sh_row, sc_row, mod_idx, wr_pad, br_pad):
    t, d = xa.shape
    tm = 512
    per = ROW_TILE // tm
    off = row0_tiles * per
    n = t // tm - off
    tl = n * tm
    small = lambda dt: jax.ShapeDtypeStruct((tl, LANES), dt)
    return pl.pallas_call(
        _router_kernel,
        grid=(n,),
        in_specs=[pl.BlockSpec((tm, d), lambda i: (i + off, 0)),
                  pl.BlockSpec((1, d), lambda i: (0, 0)),
                  pl.BlockSpec((1, 1, d), lambda i: (sh_row + mod_idx((i + off) // per), 0, 0)),
                  pl.BlockSpec((1, 1, d), lambda i: (sc_row + mod_idx((i + off) // per), 0, 0)),
                  pl.BlockSpec((d, LANES), lambda i: (0, 0)),
                  pl.BlockSpec((1, LANES), lambda i: (0, 0))],
        out_specs=[pl.BlockSpec((tm, d), lambda i: (i, 0)),
                   pl.BlockSpec((tm, LANES), lambda i: (i, 0)),
                   pl.BlockSpec((tm, LANES), lambda i: (i, 0)),
                   pl.BlockSpec((tm, LANES), lambda i: (i, 0)),
                   pl.BlockSpec((1, LANES), lambda i: (0, 0))],
        out_shape=[jax.ShapeDtypeStruct((tl, d), F32), small(jnp.int32), small(F32),
                   small(jnp.int32), jax.ShapeDtypeStruct((1, LANES), F32)],
        scratch_shapes=[pltpu.VMEM((1, LANES), F32)],
        compiler_params=_cparams(40, 1),
        name="router",
    )(xa, gain.reshape(1, d), mods, mods, wr_pad, br_pad)


DMA_UNROLL = 8


def _row_gather(idx_ref, n_rows, src_ref, dst_ref, sem):
    def body(g, carry):
        for u in range(DMA_UNROLL):
            r = g * DMA_UNROLL + u
            pltpu.make_async_copy(src_ref.at[pl.ds(idx_ref[0, 0, r], 1)],
                                  dst_ref.at[pl.ds(r, 1)], sem).start()
        return carry
    lax.fori_loop(0, n_rows // DMA_UNROLL, body, 0)


def _expert_kernel(be_ref, bv_ref, na_ref, cur_ref, nxt_ref, h_ref, w1g_ref, w1l_ref, b1g_ref,
                   b1l_ref, w2_ref, b2_ref, o_ref, gbuf, xs, act_s, sem, *, nf1):
    b = pl.program_id(0)
    j = pl.program_id(1)
    n_act = na_ref[0]
    active = b < n_act
    rows = EXPERT_ROWS
    n_sub = (bv_ref[b] + EXPERT_SUB - 1) // EXPERT_SUB
    jj = jnp.minimum(j, nf1 - 1)

    @pl.when(jnp.logical_and(active, j == 0))
    def _():
        @pl.when(b == 0)
        def _():
            _row_gather(cur_ref, rows, h_ref, gbuf, sem)

        pltpu.make_async_copy(h_ref.at[pl.ds(0, rows)], gbuf, sem).wait()
        xs[...] = gbuf[...].astype(BF16)

        @pl.when(b + 1 < n_act)
        def _():
            _row_gather(nxt_ref, rows, h_ref, gbuf, sem)

    @pl.when(jnp.logical_and(jnp.logical_not(active), j >= nf1))
    def _():
        o_ref[...] = jnp.zeros_like(o_ref)

    for m in range(1, rows // EXPERT_SUB + 1):
        r = m * EXPERT_SUB

        @pl.when(jnp.logical_and(active, jnp.logical_and(j < nf1, n_sub == m)))
        def _():
            x = xs[0:r, :]
            hg = jnp.dot(x, w1g_ref[0].astype(BF16), preferred_element_type=F32) + b1g_ref[0]
            hl = jnp.dot(x, w1l_ref[0].astype(BF16), preferred_element_type=F32) + b1l_ref[0]
            hg = jnp.minimum(hg, SWIGLU_LIMIT)
            hl = jnp.clip(hl, -SWIGLU_LIMIT, SWIGLU_LIMIT)
            act = hg * _sigmoid(SWIGLU_ALPHA * hg) * (hl + 1.0)
            act_s[jj, 0:r, :] = act.astype(BF16)

        @pl.when(jnp.logical_and(active, jnp.logical_and(j >= nf1, n_sub == m)))
        def _():
            a = jnp.concatenate([act_s[t, 0:r, :] for t in range(nf1)], axis=1)
            o_ref[0:r, :] = jnp.dot(a, w2_ref[0].astype(BF16),
                                    preferred_element_type=F32) + b2_ref[0]
            if r < rows:
                o_ref[r:rows, :] = jnp.zeros((rows - r, o_ref.shape[1]), o_ref.dtype)


def experts(h, src_tok, blk_e, blk_valid, n_active, w1, b1, w2, b2, layer):
    _, n_exp, d, f2 = w1.shape
    f = f2 // 2
    tf, tn = EXPERT_FT, EXPERT_NT
    nf1, nf2 = f // tf, d // tn
    n_blk = blk_e.shape[0]
    rows = EXPERT_ROWS

    def eb(b, be, na):
        return be[jnp.minimum(b, na[0] - 1)]

    def j1(b, j, na):
        return jnp.where(b < na[0], jnp.minimum(j, nf1 - 1), nf1 - 1)

    def j2(b, j, na):
        return jnp.where(b < na[0], jnp.maximum(j - nf1, 0), nf2 - 1)

    smem_rows = functools.partial(pl.BlockSpec, (1, 1, rows), memory_space=pltpu.SMEM)
    in_specs = [
        smem_rows(lambda b, j, be, bv, na: (b, 0, 0)),
        smem_rows(lambda b, j, be, bv, na: (jnp.minimum(b + 1, n_blk - 1), 0, 0)),
        pl.BlockSpec(memory_space=pl.ANY),
        pl.BlockSpec((None, 1, d, tf), lambda b, j, be, bv, na: (layer, eb(b, be, na), 0, j1(b, j, na))),
        pl.BlockSpec((None, 1, d, tf), lambda b, j, be, bv, na: (layer, eb(b, be, na), 0, nf1 + j1(b, j, na))),
        pl.BlockSpec((1, 1, tf), lambda b, j, be, bv, na: (eb(b, be, na), 0, j1(b, j, na))),
        pl.BlockSpec((1, 1, tf), lambda b, j, be, bv, na: (eb(b, be, na), 0, nf1 + j1(b, j, na))),
        pl.BlockSpec((None, 1, f, tn), lambda b, j, be, bv, na: (layer, eb(b, be, na), 0, j2(b, j, na))),
        pl.BlockSpec((1, 1, tn), lambda b, j, be, bv, na: (eb(b, be, na), 0, j2(b, j, na))),
    ]
    return pl.pallas_call(
        functools.partial(_expert_kernel, nf1=nf1),
        grid_spec=pltpu.PrefetchScalarGridSpec(
            num_scalar_prefetch=3,
            grid=(n_blk, nf1 + nf2),
            in_specs=in_specs,
            out_specs=pl.BlockSpec((rows, tn), lambda b, j, be, bv, na: (b, jnp.maximum(j - nf1, 0))),
            scratch_shapes=[pltpu.VMEM((rows, d), F32), pltpu.VMEM((rows, d), BF16),
                            pltpu.VMEM((nf1, rows, tf), BF16), pltpu.SemaphoreType.DMA(())]),
        out_shape=jax.ShapeDtypeStruct((n_blk * rows, d), F32),
        compiler_params=_cparams(58, 2),
        name="moe_experts",
    )(blk_e, blk_valid, n_active, src_tok, src_tok, h, w1, w1, b1.reshape(n_exp, 1, f2),
      b1.reshape(n_exp, 1, f2), w2, b2.reshape(n_exp, 1, d))


def _combine_kernel(dest_ref, nxt_ref, y_ref, x_ref, gate_ref, gt_ref, ng_ref, o_ref, buf_ref, sem,
                    *, final_norm):
    i = pl.program_id(0)
    tm = x_ref.shape[0]
    n_rows = tm * TOP_K
    slot = i % 2

    @pl.when(i == 0)
    def _():
        _row_gather(dest_ref, n_rows, y_ref, buf_ref.at[0], sem.at[0])

    @pl.when(i + 1 < pl.num_programs(0))
    def _():
        _row_gather(nxt_ref, n_rows, y_ref, buf_ref.at[1 - slot], sem.at[1 - slot])

    pltpu.make_async_copy(y_ref.at[pl.ds(0, n_rows)], buf_ref.at[slot], sem.at[slot]).wait()
    gate = gate_ref[...]
    acc = jnp.zeros(x_ref.shape, F32)
    for kk in range(TOP_K):
        acc = acc + gate[:, kk:kk + 1] * buf_ref[slot, pl.ds(kk * tm, tm), :]
    xn = x_ref[...] + gt_ref[0] * acc
    if final_norm:
        xn = xn * lax.rsqrt(jnp.mean(xn * xn, axis=-1, keepdims=True) + EPS) * ng_ref[...]
    o_ref[...] = xn


def combine(y_sorted, dest_km, xa, row0_tiles, gate, mods, gt_row, mod_idx, norm_g, final_norm):
    t, d = xa.shape
    tm = 256
    per = ROW_TILE // tm
    off = row0_tiles * per
    n = t // tm - off
    out_rows = n * tm if final_norm else t
    out_off = 0 if final_norm else off
    kernel = functools.partial(_combine_kernel, final_norm=final_norm)
    call = pl.pallas_call(
        kernel,
        grid=(n,),
        in_specs=[pl.BlockSpec((1, 1, TOP_K * tm), lambda i: (i, 0, 0), memory_space=pltpu.SMEM),
                  pl.BlockSpec((1, 1, TOP_K * tm), lambda i: (jnp.minimum(i + 1, n - 1), 0, 0),
                               memory_space=pltpu.SMEM),
                  pl.BlockSpec(memory_space=pl.ANY),
                  pl.BlockSpec((tm, d), lambda i: (i + off, 0)),
                  pl.BlockSpec((tm, LANES), lambda i: (i, 0)),
                  pl.BlockSpec((1, 1, d), lambda i: (gt_row + mod_idx((i + off) // per), 0, 0)),
                  pl.BlockSpec((1, d), lambda i: (0, 0))],
        out_specs=pl.BlockSpec((tm, d), lambda i: (i + out_off, 0)),
        out_shape=jax.ShapeDtypeStruct((out_rows, d), F32),
        scratch_shapes=[pltpu.VMEM((2, TOP_K * tm, d), F32), pltpu.SemaphoreType.DMA((2,))],
        compiler_params=_cparams(40, 1),
        name="moe_combine",
    )
    return call(dest_km, dest_km, y_sorted, xa, gate, mods, norm_g.reshape(1, d))


def _routing_tables(eidx, rank, counts, n_blk):
    t = eidx.shape[0]
    rows = EXPERT_ROWS
    padded = (counts + rows - 1) // rows * rows
    p_ends = jnp.cumsum(padded)
    p_starts = p_ends - padded
    starts = jnp.cumsum(counts) - counts
    dest = p_starts[eidx] + rank
    n_active = (p_ends[-1] // rows).astype(jnp.int32).reshape(1)
    blk_start = jnp.arange(n_blk, dtype=jnp.int32) * rows
    blk_e = jnp.minimum(jnp.searchsorted(p_ends, blk_start, side="right"), N_EXPERTS - 1).astype(jnp.int32)
    blk_valid = jnp.clip(counts[blk_e] - (blk_start - p_starts[blk_e]), 0, rows).astype(jnp.int32)
    order = jnp.argsort(eidx.reshape(-1), stable=True).astype(jnp.int32)
    tok_sorted = order // TOP_K
    slot = jnp.arange(n_blk * rows, dtype=jnp.int32)
    slot_e = jnp.repeat(blk_e, rows)
    within = slot - p_starts[slot_e]
    src = jnp.where(within < counts[slot_e],
                    tok_sorted[jnp.clip(starts[slot_e] + within, 0, t * TOP_K - 1)], 0)
    return dest, src.reshape(n_blk, 1, rows).astype(jnp.int32), blk_e, blk_valid, n_active


def moe_ffn(xa, row0_tiles, gain, mods, base_row, mod_idx, wr_pad, br_pad, w1, b1, w2, b2, layer,
            norm_final, final_norm):
    n_b = 8
    h, eidx, gate, rank, cnt = router(xa, row0_tiles, gain, mods, base_row + 3 * n_b,
                                      base_row + 4 * n_b, mod_idx, wr_pad, br_pad)
    tl = h.shape[0]
    eidx4 = eidx[:, :TOP_K]
    rank4 = rank[:, :TOP_K]
    counts = cnt[0, :N_EXPERTS].astype(jnp.int32)
    n_blk = (tl * TOP_K + N_EXPERTS * (EXPERT_ROWS - 1)) // EXPERT_ROWS
    dest, src, blk_e, blk_valid, n_active = _routing_tables(eidx4, rank4, counts, n_blk)
    ys = experts(h, src, blk_e, blk_valid, n_active, w1, b1, w2, b2, layer)
    tmc = 256
    dest_km = dest.reshape(tl // tmc, tmc, TOP_K).transpose(0, 2, 1).reshape(tl // tmc, 1, TOP_K * tmc)
    return combine(ys, dest_km.astype(jnp.int32), xa, row0_tiles, gate, mods, base_row + 5 * n_b,
                   mod_idx, norm_final, final_norm)


def kernel(x, c, ctx, c_ctx, norm_mix, norm_ffn, w_ada, b_ada, w_in, conv_w, a_ws, a_bias, gla_wf, gla_bf, gla_wb, gla_bb, gla_norm, p_a, p_b, w_out, w_router, b_router, w1, b1, w2, b2, norm_final):
    n_b, seq, d = x.shape
    n_ctx = ctx.shape[1]
    depth = w_ada.shape[0]
    assert n_b * n_ctx == ROW_TILE and seq % ROW_TILE == 0 and n_b < 8
    tiles_per_batch = seq // ROW_TILE
    a_width = a_ws.shape[1] * a_ws.shape[2]
    dk = gla_wf.shape[2]
    dv = d

    def mod_idx(tile):
        return jnp.where(tile == 0, n_b, (tile - 1) // tiles_per_batch)

    xa = (ctx.reshape(n_b * n_ctx, d), x.reshape(n_b * seq, d))
    cond = jnp.zeros((8, d), F32).at[:n_b].set(c).at[n_b].set(c_ctx)
    mods_all = ada_mods(cond, w_ada, b_ada)
    mods = mods_all.reshape(depth, 8, N_MOD, d).transpose(0, 2, 1, 3).reshape(depth * N_MOD * 8, 1, d)

    c_u, c_q, c_r, c_lr, c_g = 0, 2 * a_width, 2 * a_width + 2 * dk + dv, 2 * a_width + 2 * dk + 2 * dv, \
        2 * a_width + 2 * dk + 2 * dv + 2 * GLA_RANK
    out = None
    for l in range(depth):
        last = l == depth - 1
        base = l * N_MOD * 8
        w_lr = jnp.zeros((1, d, LANES), F32).at[0, :, :2 * GLA_RANK].set(
            w_in[l, :, c_lr:c_lr + 2 * GLA_RANK])
        w_g = w_in[l:l + 1, :, c_g:]
        wf_pad = jnp.zeros((LANES, dk), F32).at[:GLA_RANK].set(gla_wf[l])
        wb_pad = jnp.zeros((LANES, dk), F32).at[GLA_RANK:2 * GLA_RANK].set(gla_wb[l])

        h = norm_mod(xa, norm_mix[l], mods, base + 0 * 8, base + 1 * 8, mod_idx)
        uv = matmul_act(h, w_in, l, c_u, 2 * a_width, act="gelu")
        qkv = matmul_act(h, w_in, l, c_q, 2 * dk + dv, act="conv", conv_w=conv_w[l],
                         ctx_seg=n_ctx, q_cols=dk, q_scale=float(dk // GLA_HEADS) ** -0.5)
        r_act = matmul_act(h, w_in, l, c_r, dv, act="silu")
        lr = matmul_act(h, w_lr, 0, 0, LANES, act="none", out_dtype=F32, tn=LANES)
        gates = matmul_act(h, w_g, 0, 0, 2 * d, act="sigmoid")

        n_ctx_blocks = n_b * n_ctx // GLA_BLOCK
        o_f = gla_scan(qkv, lr, None, None, wf_pad, gla_bf[l].reshape(1, dk), None,
                       n_b, n_ctx_blocks, reverse=False)
        y_gla = gla_scan(qkv, lr, r_act, o_f, wb_pad, gla_bb[l].reshape(1, dk),
                         gla_norm[l].reshape(1, -1), n_b, n_ctx_blocks, reverse=True)
        a = gmlp(uv, a_ws[l], a_bias[l].T)
        mrg = merge_proj(a, y_gla, gates, p_a, p_b, l)
        xa = out_proj(mrg, w_out, l, xa, mods, base + 2 * 8, mod_idx)

        wr_pad = jnp.zeros((d, LANES), F32).at[:, :N_EXPERTS].set(w_router[l])
        br_pad = jnp.full((1, LANES), -jnp.inf, F32).at[0, :N_EXPERTS].set(b_router[l])
        res = moe_ffn(xa, 1 if last else 0, norm_ffn[l], mods, base, mod_idx, wr_pad, br_pad,
                      w1, b1[l], w2, b2[l], l, norm_final, last)
        if last:
            out = res
        else:
            xa = res
    return out.reshape(n_b, seq, d)
```

```python
import functools

import jax
import jax.numpy as jnp
from jax import lax
from jax.experimental import pallas as pl
from jax.experimental.pallas import tpu as pltpu

F32 = jnp.float32
BF16 = jnp.bfloat16

GRID_W = 64
EPS = 1e-6
N_MOD = 6
A_GROUPS = 8
A_GROUP_W = 128
A_CHUNK = 128
GLA_HEADS = 4
GLA_RANK = 16
GLA_TAU = 16.0
GLA_CHUNK = 64
N_EXPERTS = 32
TOP_K = 4
SWIGLU_LIMIT = 7.0
SWIGLU_ALPHA = 1.702

LANES = 128
ROW_TILE = 1024
COL_TILE = 512
GLA_BLOCK = 256
GLA_HEADS_PER_STEP = 2
EXPERT_ROWS = 1024
EXPERT_SUB = 256
EXPERT_FT = 512
EXPERT_NT = 512
MIB = 1024 * 1024


def _cparams(vmem_mib, n_axes):
    return pltpu.CompilerParams(
        dimension_semantics=("arbitrary",) * n_axes,
        vmem_limit_bytes=int(vmem_mib * MIB))


def _sigmoid(t):
    return 0.5 * (1.0 + jnp.tanh(0.5 * t))


def _silu(t):
    return t * _sigmoid(t)


def _gelu_tanh(t):
    return 0.5 * t * (1.0 + jnp.tanh(0.7978845608028654 * (t + 0.044715 * (t * t * t))))


def _log_sigmoid(z):
    return jnp.minimum(z, 0.0) - jnp.log(1.0 + jnp.exp(-jnp.abs(z)))


def _ada_kernel(c_ref, w_ref, b_ref, o_ref):
    s = _silu(c_ref[...]).astype(BF16)
    w = w_ref[0].astype(BF16)
    o_ref[0] = jnp.dot(s, w, preferred_element_type=F32) + b_ref[0]


def ada_mods(cond, w_ada, b_ada):
    n_layer, d, n = w_ada.shape
    tn = 1024
    return pl.pallas_call(
        _ada_kernel,
        grid=(n_layer, n // tn),
        in_specs=[pl.BlockSpec((8, d), lambda l, j: (0, 0)),
                  pl.BlockSpec((1, d, tn), lambda l, j: (l, 0, j)),
                  pl.BlockSpec((1, 1, tn), lambda l, j: (l, 0, j))],
        out_specs=pl.BlockSpec((1, 8, tn), lambda l, j: (l, 0, j)),
        out_shape=jax.ShapeDtypeStruct((n_layer, 8, n), F32),
        compiler_params=_cparams(32, 2),
        name="ada_mods",
    )(cond, w_ada, b_ada.reshape(n_layer, 1, n))


def _stream_specs(stream, tm, tn, row_of, col_of):
    if not isinstance(stream, tuple):
        return [pl.BlockSpec((tm, tn), lambda *g: (row_of(*g), col_of(*g)))], [stream], 0
    ctx2d, lat2d = stream
    nct = ctx2d.shape[0] // tm
    specs = [pl.BlockSpec((tm, tn), lambda *g: (jnp.minimum(row_of(*g), nct - 1), col_of(*g))),
             pl.BlockSpec((tm, tn), lambda *g: (jnp.maximum(row_of(*g) - nct, 0), col_of(*g)))]
    return specs, [ctx2d, lat2d], nct


def _stream_tile(refs, tile, n_ctx_tiles):
    if len(refs) == 1:
        return refs[0][...]
    return jnp.where(tile < n_ctx_tiles, refs[0][...], refs[1][...])


def _norm_mod_kernel(*refs, n_ctx_tiles):
    g_ref, sh_ref, sc_ref, o_ref = refs[-4:]
    x = _stream_tile(refs[:-4], pl.program_id(0), n_ctx_tiles)
    r = lax.rsqrt(jnp.mean(x * x, axis=-1, keepdims=True) + EPS)
    h = (x * r) * g_ref[...]
    h = h * (1.0 + sc_ref[0]) + sh_ref[0]
    o_ref[...] = h.astype(o_ref.dtype)


def norm_mod(stream, gain, mods, sh_row, sc_row, mod_idx):
    t = sum(s.shape[0] for s in stream) if isinstance(stream, tuple) else stream.shape[0]
    d = gain.shape[0]
    tm = 512
    per = ROW_TILE // tm
    specs, arrays, nct = _stream_specs(stream, tm, d, lambda i: i, lambda i: 0)
    return pl.pallas_call(
        functools.partial(_norm_mod_kernel, n_ctx_tiles=nct),
        grid=(t // tm,),
        in_specs=specs + [
            pl.BlockSpec((1, d), lambda i: (0, 0)),
            pl.BlockSpec((1, 1, d), lambda i: (sh_row + mod_idx(i // per), 0, 0)),
            pl.BlockSpec((1, 1, d), lambda i: (sc_row + mod_idx(i // per), 0, 0))],
        out_specs=pl.BlockSpec((tm, d), lambda i: (i, 0)),
        out_shape=jax.ShapeDtypeStruct((t, d), BF16),
        compiler_params=_cparams(40, 1),
        name="norm_mod",
    )(*arrays, gain.reshape(1, d), mods, mods)


def _mm_kernel(*refs, act, ctx_seg, q_tiles, q_scale):
    if act == "conv":
        x_ref, w_ref, cw_ref, o_ref, wb_ref = refs
    else:
        x_ref, w_ref, o_ref, wb_ref = refs
    j = pl.program_id(0)
    i = pl.program_id(1)

    @pl.when(i == 0)
    def _():
        wb_ref[...] = w_ref[...].T.astype(BF16)

    y = jnp.dot(x_ref[...], wb_ref[...], preferred_element_type=F32)
    if act == "gelu":
        y = _gelu_tanh(y)
    elif act == "silu":
        y = _silu(y.astype(o_ref.dtype))
    elif act == "sigmoid":
        y = _sigmoid(y.astype(o_ref.dtype))
    elif act == "conv":
        seg = jnp.where(i == 0, ctx_seg, GRID_W)
        pos = lax.broadcasted_iota(jnp.int32, y.shape, 0) & (seg - 1)
        prev = jnp.where(pos == 0, 0.0, pltpu.roll(y, 1, 0))
        nxt = jnp.where(pos == seg - 1, 0.0, pltpu.roll(y, y.shape[0] - 1, 0))
        cw = cw_ref[...]
        y = _silu((cw[0:1] * prev + cw[1:2] * y + cw[2:3] * nxt).astype(o_ref.dtype))
        y = y * jnp.where(j < q_tiles, q_scale, 1.0).astype(o_ref.dtype)
    o_ref[...] = y.astype(o_ref.dtype)


def matmul_act(h, w_t, layer, col0, ncols, act="none", out_dtype=BF16, conv_w=None,
               ctx_seg=256, q_cols=0, q_scale=1.0, tn=COL_TILE):
    t, k = h.shape
    tm = ROW_TILE
    off = col0 // tn
    in_specs = [pl.BlockSpec((tm, k), lambda j, i: (i, 0)),
                pl.BlockSpec((None, tn, k), lambda j, i: (layer, j + off, 0))]
    args = [h, w_t]
    if act == "conv":
        in_specs.append(pl.BlockSpec((3, tn), lambda j, i: (0, j)))
        args.append(conv_w)
    return pl.pallas_call(
        functools.partial(_mm_kernel, act=act, ctx_seg=ctx_seg, q_tiles=q_cols // tn,
                          q_scale=q_scale),
        grid=(ncols // tn, t // tm),
        in_specs=in_specs,
        out_specs=pl.BlockSpec((tm, tn), lambda j, i: (i, j)),
        out_shape=jax.ShapeDtypeStruct((t, ncols), out_dtype),
        scratch_shapes=[pltpu.VMEM((k, tn), BF16)],
        compiler_params=_cparams(48, 2),
        name="matmul_" + act,
    )(*args)


def _gla_block(q, k, v, g, s_ref, tri, same, eye, reverse):
    c = GLA_CHUNK
    n_chunk = q.shape[0] // c
    cum = jnp.dot(tri, g, preferred_element_type=F32)
    tot = jnp.dot(same, g, preferred_element_type=F32)
    qt = (q * jnp.exp(cum)).astype(BF16)
    kt = (k * jnp.exp(-cum)).astype(BF16)
    ke = (k * jnp.exp(tot - cum)).astype(BF16)
    att = lax.dot_general(qt, kt, (((1,), (1,)), ((), ())), preferred_element_type=F32)
    att = jnp.where(tri > 0.0, att, 0.0).astype(BF16)
    o_intra = jnp.dot(att, v, preferred_element_type=F32)
    dk = g.shape[1]
    outs = [None] * n_chunk
    for ci in (range(n_chunk - 1, -1, -1) if reverse else range(n_chunk)):
        r0, r1 = ci * c, (ci + 1) * c
        s = s_ref[...]
        outs[ci] = o_intra[r0:r1] + jnp.dot(qt[r0:r1], s.astype(BF16), preferred_element_type=F32)
        tot_col = jnp.sum(jnp.where(eye, jnp.broadcast_to(tot[r0:r0 + 1], (dk, dk)), 0.0),
                          axis=1, keepdims=True)
        kv = lax.dot_general(ke[r0:r1], v[r0:r1], (((0,), (0,)), ((), ())),
                             preferred_element_type=F32)
        s_ref[...] = s * jnp.exp(tot_col) + kv
    return jnp.concatenate(outs, axis=0)


def _gla_kernel(*refs, reverse):
    if reverse:
        (q_ref, k_ref, v_ref, lr_ref, w_ref, b_ref, of_ref, r_ref, ng_ref, y_ref, s_ref) = refs
    else:
        (q_ref, k_ref, v_ref, lr_ref, w_ref, b_ref, o_ref, s_ref) = refs
    step = pl.program_id(2)

    @pl.when(step == 0)
    def _():
        s_ref[...] = jnp.zeros_like(s_ref)

    c = GLA_CHUNK
    n_head, dk, dv = s_ref.shape
    n_row = q_ref.shape[0]
    z = jnp.dot(lr_ref[...], w_ref[...], preferred_element_type=F32) + b_ref[...]
    g = _log_sigmoid(z) * (1.0 / GLA_TAU)
    row = lax.broadcasted_iota(jnp.int32, (n_row, n_row), 0)
    col = lax.broadcasted_iota(jnp.int32, (n_row, n_row), 1)
    shift = c.bit_length() - 1
    same_b = (row >> shift) == (col >> shift)
    same = same_b.astype(F32)
    tri = jnp.logical_and(same_b, (col >= row) if reverse else (col <= row)).astype(F32)
    eye = (lax.broadcasted_iota(jnp.int32, (dk, dk), 0)
           == lax.broadcasted_iota(jnp.int32, (dk, dk), 1))
    for hh in range(n_head):
        kc = pl.ds(hh * dk, dk)
        vc = pl.ds(hh * dv, dv)
        o = _gla_block(q_ref[:, kc].astype(F32), k_ref[:, kc].astype(F32), v_ref[:, vc],
                       g[:, hh * dk:(hh + 1) * dk], s_ref.at[hh], tri, same, eye, reverse)
        if reverse:
            o = o + of_ref[:, vc]
            o = o * lax.rsqrt(jnp.mean(o * o, axis=-1, keepdims=True) + EPS)
            y_ref[:, vc] = (o * ng_ref[...] * r_ref[:, vc].astype(F32)).astype(y_ref.dtype)
        else:
            o_ref[:, vc] = o


def gla_scan(qkv, lr, r_act, o_fwd, w_pad, bias, norm_g, n_batch, n_ctx_blocks, reverse):
    t = qkv.shape[0]
    dkh = w_pad.shape[1] // GLA_HEADS
    dv = qkv.shape[1] - 2 * w_pad.shape[1]
    dvh = dv // GLA_HEADS
    blk = GLA_BLOCK
    ctx_per_batch = n_ctx_blocks // n_batch
    assert ctx_per_batch == 1
    lat_per_batch = (t // blk - n_ctx_blocks) // n_batch
    n_step = ctx_per_batch + lat_per_batch

    def row_blk(b, s):
        if reverse:
            lat = n_ctx_blocks + lat_per_batch * b + (lat_per_batch - s)
        else:
            lat = n_ctx_blocks + lat_per_batch * b + (s - 1)
        return jnp.where(s == 0, b, lat)

    hp = GLA_HEADS_PER_STEP
    kw, vw = hp * dkh, hp * dvh
    k_off = w_pad.shape[1] // kw
    v_off = 2 * w_pad.shape[1] // vw
    in_specs = [pl.BlockSpec((blk, kw), lambda b, h, s: (row_blk(b, s), h)),
                pl.BlockSpec((blk, kw), lambda b, h, s: (row_blk(b, s), k_off + h)),
                pl.BlockSpec((blk, vw), lambda b, h, s: (row_blk(b, s), v_off + h)),
                pl.BlockSpec((blk, LANES), lambda b, h, s: (row_blk(b, s), 0)),
                pl.BlockSpec((LANES, kw), lambda b, h, s: (0, h)),
                pl.BlockSpec((1, kw), lambda b, h, s: (0, h))]
    args = [qkv, qkv, qkv, lr, w_pad, bias]
    if reverse:
        in_specs += [pl.BlockSpec((blk, vw), lambda b, h, s: (row_blk(b, s), h)),
                     pl.BlockSpec((blk, vw), lambda b, h, s: (row_blk(b, s), h)),
                     pl.BlockSpec((1, dvh), lambda b, h, s: (0, 0))]
        args += [o_fwd, r_act, norm_g]
        out_dtype = BF16
    else:
        out_dtype = F32
    return pl.pallas_call(
        functools.partial(_gla_kernel, reverse=reverse),
        grid=(n_batch, GLA_HEADS // hp, n_step),
        in_specs=in_specs,
        out_specs=pl.BlockSpec((blk, vw), lambda b, h, s: (row_blk(b, s), h)),
        out_shape=jax.ShapeDtypeStruct((t, dv), out_dtype),
        scratch_shapes=[pltpu.VMEM((hp, dkh, dvh), F32)],
        compiler_params=_cparams(32, 3),
        name="gla_bwd" if reverse else "gla_fwd",
    )(*args)


def _gmlp_kernel(u_ref, v_ref, ws_ref, bt_ref, o_ref):
    v = v_ref[...].astype(F32)
    mu = jnp.mean(v, axis=-1, keepdims=True)
    vc = v - mu
    var = jnp.mean(vc * vc, axis=-1, keepdims=True)
    vn = (vc * lax.rsqrt(var + EPS)).astype(BF16)
    n_chunk = v.shape[0] // A_CHUNK
    for gi in range(A_GROUPS):
        wsg = ws_ref[gi].astype(BF16)
        bias = bt_ref[:, gi:gi + 1]
        cols = slice(gi * A_GROUP_W, (gi + 1) * A_GROUP_W)
        for n in range(n_chunk):
            rows = slice(n * A_CHUNK, (n + 1) * A_CHUNK)
            s = jnp.dot(wsg, vn[rows, cols], preferred_element_type=F32) + bias
            o_ref[rows, cols] = (u_ref[rows, cols].astype(F32) * s).astype(o_ref.dtype)


def gmlp(uv, a_ws, a_bias_t):
    t = uv.shape[0]
    aw = uv.shape[1] // 2
    tm = 512
    return pl.pallas_call(
        _gmlp_kernel,
        grid=(t // tm,),
        in_specs=[pl.BlockSpec((tm, aw), lambda i: (i, 0)),
                  pl.BlockSpec((tm, aw), lambda i: (i, 1)),
                  pl.BlockSpec(a_ws.shape, lambda i: (0, 0, 0)),
                  pl.BlockSpec(a_bias_t.shape, lambda i: (0, 0))],
        out_specs=pl.BlockSpec((tm, aw), lambda i: (i, 0)),
        out_shape=jax.ShapeDtypeStruct((t, aw), BF16),
        compiler_params=_cparams(32, 1),
        name="gmlp",
    )(uv, uv, a_ws, a_bias_t)


def _merge_kernel(a_ref, y_ref, ga_ref, gb_ref, pa_ref, pb_ref, o_ref, pab_ref, pbb_ref):
    @pl.when(pl.program_id(1) == 0)
    def _():
        pab_ref[...] = pa_ref[...].astype(BF16)
        pbb_ref[...] = pb_ref[...].astype(BF16)

    ya = jnp.dot(a_ref[...], pab_ref[...], preferred_element_type=F32)
    yb = jnp.dot(y_ref[...], pbb_ref[...], preferred_element_type=F32)
    o_ref[...] = (ga_ref[...].astype(F32) * ya + gb_ref[...].astype(F32) * yb).astype(o_ref.dtype)


def merge_proj(a, yb, gates, p_a, p_b, layer):
    t, d = yb.shape
    tm, tn = ROW_TILE, COL_TILE
    nj = d // tn
    ka, kb = p_a.shape[1], p_b.shape[1]
    return pl.pallas_call(
        _merge_kernel,
        grid=(nj, t // tm),
        in_specs=[pl.BlockSpec((tm, ka), lambda j, i: (i, 0)),
                  pl.BlockSpec((tm, kb), lambda j, i: (i, 0)),
                  pl.BlockSpec((tm, tn), lambda j, i: (i, j)),
                  pl.BlockSpec((tm, tn), lambda j, i: (i, nj + j)),
                  pl.BlockSpec((None, ka, tn), lambda j, i: (layer, 0, j)),
                  pl.BlockSpec((None, kb, tn), lambda j, i: (layer, 0, j))],
        out_specs=pl.BlockSpec((tm, tn), lambda j, i: (i, j)),
        out_shape=jax.ShapeDtypeStruct((t, d), BF16),
        scratch_shapes=[pltpu.VMEM((ka, tn), BF16), pltpu.VMEM((kb, tn), BF16)],
        compiler_params=_cparams(48, 2),
        name="merge_proj",
    )(a, yb, gates, gates, p_a, p_b)


def _out_proj_kernel(*refs, n_ctx_tiles):
    m_ref, w_ref = refs[:2]
    gt_ref, o_ref, wb_ref = refs[-3:]
    i = pl.program_id(1)

    @pl.when(i == 0)
    def _():
        wb_ref[...] = w_ref[...].astype(BF16)

    y = jnp.dot(m_ref[...], wb_ref[...], preferred_element_type=F32)
    o_ref[...] = _stream_tile(refs[2:-3], i, n_ctx_tiles) + gt_ref[0] * y


def out_proj(mrg, w_out, layer, stream, mods, gt_row, mod_idx):
    t, d = mrg.shape
    tm, tn = ROW_TILE, COL_TILE
    specs, arrays, nct = _stream_specs(stream, tm, tn, lambda j, i: i, lambda j, i: j)
    return pl.pallas_call(
        functools.partial(_out_proj_kernel, n_ctx_tiles=nct),
        grid=(d // tn, t // tm),
        in_specs=[pl.BlockSpec((tm, d), lambda j, i: (i, 0)),
                  pl.BlockSpec((None, d, tn), lambda j, i: (layer, 0, j))] + specs + [
                  pl.BlockSpec((1, 1, tn), lambda j, i: (gt_row + mod_idx(i), 0, j))],
        out_specs=pl.BlockSpec((tm, tn), lambda j, i: (i, j)),
        out_shape=jax.ShapeDtypeStruct((t, d), F32),
        scratch_shapes=[pltpu.VMEM((d, tn), BF16)],
        compiler_params=_cparams(48, 2),
        name="out_proj",
    )(mrg, w_out, *arrays, mods)


def _router_kernel(x_ref, g_ref, sh_ref, sc_ref, wr_ref, br_ref,
                   h_ref, idx_ref, gate_ref, rank_ref, cnt_ref, carry_ref):
    @pl.when(pl.program_id(0) == 0)
    def _():
        carry_ref[...] = jnp.zeros_like(carry_ref)

    x = x_ref[...]
    r = lax.rsqrt(jnp.mean(x * x, axis=-1, keepdims=True) + EPS)
    h = (x * r) * g_ref[...]
    h = h * (1.0 + sc_ref[0]) + sh_ref[0]
    n_piece = h.shape[1] // LANES
    for p in range(n_piece):
        h_ref[pl.ds(p, h.shape[0], stride=n_piece), :] = h[:, p * LANES:(p + 1) * LANES]
    logits = jnp.dot(h, wr_ref[...], preferred_element_type=F32) + br_ref[...]
    tm = x.shape[0]
    lane = lax.broadcasted_iota(jnp.int32, (tm, LANES), 1)
    lane_f = lane.astype(F32)
    member = jnp.zeros((tm, LANES), F32)
    vals, sels = [], []
    idx_out = jnp.zeros((tm, LANES), F32)
    for kk in range(TOP_K):
        m = jnp.max(logits, axis=1, keepdims=True)
        ik = jnp.min(jnp.where(logits == m, lane_f, float(LANES)), axis=1, keepdims=True)
        sel = lane_f == ik
        logits = jnp.where(sel, -jnp.inf, logits)
        member = member + sel.astype(F32)
        idx_out = jnp.where(lane == kk, ik, idx_out)
        vals.append(m)
        sels.append(sel)
    es = [jnp.exp(v - vals[0]) for v in vals]
    denom = es[0] + es[1] + es[2] + es[3]
    gate_out = jnp.zeros((tm, LANES), F32)
    for kk in range(TOP_K):
        gate_out = jnp.where(lane == kk, es[kk] / denom, gate_out)
    rr = lax.broadcasted_iota(jnp.int32, (tm, tm), 0)
    cc = lax.broadcasted_iota(jnp.int32, (tm, tm), 1)
    strict = (cc < rr).astype(BF16)
    before = jnp.dot(strict, member.astype(BF16), preferred_element_type=F32) + carry_ref[...]
    rank_out = jnp.zeros((tm, LANES), F32)
    for kk in range(TOP_K):
        rk = jnp.sum(jnp.where(sels[kk], before, 0.0), axis=1, keepdims=True)
        rank_out = jnp.where(lane == kk, rk, rank_out)
    carry_ref[...] = carry_ref[...] + jnp.sum(member, axis=0, keepdims=True)
    idx_ref[...] = idx_out.astype(jnp.int32)
    gate_ref[...] = gate_out
    rank_ref[...] = rank_out.astype(jnp.int32)
    cnt_ref[...] = carry_ref[...]


def router(xa, row0_tiles, gain, mods, sh_row, sc_row, mod_idx, wr_pad, br_pad):
    t, d = xa.shape
    tm = 512
    per = ROW_TILE // tm
    off = row0_tiles * per
    n = t // tm - off
    tl = n * tm
    small = lambda dt: jax.ShapeDtypeStruct((tl, LANES), dt)
    return pl.pallas_call(
        _router_kernel,
        grid=(n,),
        in_specs=[pl.BlockSpec((tm, d), lambda i: (i + off, 0)),
                  pl.BlockSpec((1, d), lambda i: (0, 0)),
                  pl.BlockSpec((1, 1, d), lambda i: (sh_row + mod_idx((i + off) // per), 0, 0)),
                  pl.BlockSpec((1, 1, d), lambda i: (sc_row + mod_idx((i + off) // per), 0, 0)),
                  pl.BlockSpec((d, LANES), lambda i: (0, 0)),
                  pl.BlockSpec((1, LANES), lambda i: (0, 0))],
        out_specs=[pl.BlockSpec((tm * (d // LANES), LANES), lambda i: (i, 0)),
                   pl.BlockSpec((tm, LANES), lambda i: (i, 0)),
                   pl.BlockSpec((tm, LANES), lambda i: (i, 0)),
                   pl.BlockSpec((tm, LANES), lambda i: (i, 0)),
                   pl.BlockSpec((1, LANES), lambda i: (0, 0))],
        out_shape=[jax.ShapeDtypeStruct((tl * (d // LANES), LANES), F32), small(jnp.int32), small(F32),
                   small(jnp.int32), jax.ShapeDtypeStruct((1, LANES), F32)],
        scratch_shapes=[pltpu.VMEM((1, LANES), F32)],
        compiler_params=_cparams(40, 1),
        name="router",
    )(xa, gain.reshape(1, d), mods, mods, wr_pad, br_pad)


DMA_UNROLL = 8


def _row_gather(idx_ref, n_rows, src_ref, dst_ref, sem, run=1):
    pitch = dst_ref.shape[0] // n_rows

    def body(g, carry):
        for u in range(DMA_UNROLL):
            r = g * DMA_UNROLL + u
            first = idx_ref[0, 0, r] if run == 1 else pl.multiple_of(idx_ref[0, 0, r], run)
            pltpu.make_async_copy(src_ref.at[pl.ds(first, run)],
                                  dst_ref.at[pl.ds(r * pitch, run)], sem).start()
        return carry
    lax.fori_loop(0, n_rows // DMA_UNROLL, body, 0)


def _expert_kernel(be_ref, bv_ref, na_ref, cur_ref, nxt_ref, h_ref, w1g_ref, w1l_ref, b1g_ref,
                   b1l_ref, w2_ref, b2_ref, o_ref, gbuf, xs, act_s, sem, *, nf1):
    b = pl.program_id(0)
    j = pl.program_id(1)
    n_act = na_ref[0]
    active = b < n_act
    rows = EXPERT_ROWS
    n_sub = (bv_ref[b] + EXPERT_SUB - 1) // EXPERT_SUB
    jj = jnp.minimum(j, nf1 - 1)

    n_piece = xs.shape[1] // LANES
    pitch = gbuf.shape[0] // rows

    @pl.when(jnp.logical_and(active, j == 0))
    def _():
        @pl.when(b == 0)
        def _():
            _row_gather(cur_ref, rows, h_ref, gbuf, sem, run=n_piece)

        pltpu.make_async_copy(h_ref.at[pl.ds(0, rows * n_piece)],
                              gbuf.at[pl.ds(0, rows * n_piece)], sem).wait()
        for p in range(n_piece):
            xs[:, p * LANES:(p + 1) * LANES] = gbuf[pl.ds(p, rows, stride=pitch), :].astype(BF16)

        @pl.when(b + 1 < n_act)
        def _():
            _row_gather(nxt_ref, rows, h_ref, gbuf, sem, run=n_piece)

    @pl.when(jnp.logical_and(jnp.logical_not(active), j >= nf1))
    def _():
        o_ref[...] = jnp.zeros_like(o_ref)

    for m in range(1, rows // EXPERT_SUB + 1):
        r = m * EXPERT_SUB

        @pl.when(jnp.logical_and(active, jnp.logical_and(j < nf1, n_sub == m)))
        def _():
            x = xs[0:r, :]
            hg = jnp.dot(x, w1g_ref[0].astype(BF16), preferred_element_type=F32) + b1g_ref[0]
            hl = jnp.dot(x, w1l_ref[0].astype(BF16), preferred_element_type=F32) + b1l_ref[0]
            hg = jnp.minimum(hg, SWIGLU_LIMIT)
            hl = jnp.clip(hl, -SWIGLU_LIMIT, SWIGLU_LIMIT)
            act = hg * _sigmoid(SWIGLU_ALPHA * hg) * (hl + 1.0)
            act_s[jj, 0:r, :] = act.astype(BF16)

        @pl.when(jnp.logical_and(active, jnp.logical_and(j >= nf1, n_sub == m)))
        def _():
            a = jnp.concatenate([act_s[t, 0:r, :] for t in range(nf1)], axis=1)
            o_ref[0:r, :] = jnp.dot(a, w2_ref[0].astype(BF16),
                                    preferred_element_type=F32) + b2_ref[0]
            if r < rows:
                o_ref[r:rows, :] = jnp.zeros((rows - r, o_ref.shape[1]), o_ref.dtype)


def experts(h, src_tok, blk_e, blk_valid, n_active, w1, b1, w2, b2, layer):
    _, n_exp, d, f2 = w1.shape
    f = f2 // 2
    tf, tn = EXPERT_FT, EXPERT_NT
    nf1, nf2 = f // tf, d // tn
    n_blk = blk_e.shape[0]
    rows = EXPERT_ROWS

    def eb(b, be, na):
        return be[jnp.minimum(b, na[0] - 1)]

    def j1(b, j, na):
        return jnp.where(b < na[0], jnp.minimum(j, nf1 - 1), nf1 - 1)

    def j2(b, j, na):
        return jnp.where(b < na[0], jnp.maximum(j - nf1, 0), nf2 - 1)

    smem_rows = functools.partial(pl.BlockSpec, (1, 1, rows), memory_space=pltpu.SMEM)
    in_specs = [
        smem_rows(lambda b, j, be, bv, na: (b, 0, 0)),
        smem_rows(lambda b, j, be, bv, na: (jnp.minimum(b + 1, n_blk - 1), 0, 0)),
        pl.BlockSpec(memory_space=pl.ANY),
        pl.BlockSpec((None, 1, d, tf), lambda b, j, be, bv, na: (layer, eb(b, be, na), 0, j1(b, j, na))),
        pl.BlockSpec((None, 1, d, tf), lambda b, j, be, bv, na: (layer, eb(b, be, na), 0, nf1 + j1(b, j, na))),
        pl.BlockSpec((1, 1, tf), lambda b, j, be, bv, na: (eb(b, be, na), 0, j1(b, j, na))),
        pl.BlockSpec((1, 1, tf), lambda b, j, be, bv, na: (eb(b, be, na), 0, nf1 + j1(b, j, na))),
        pl.BlockSpec((None, 1, f, tn), lambda b, j, be, bv, na: (layer, eb(b, be, na), 0, j2(b, j, na))),
        pl.BlockSpec((1, 1, tn), lambda b, j, be, bv, na: (eb(b, be, na), 0, j2(b, j, na))),
    ]
    return pl.pallas_call(
        functools.partial(_expert_kernel, nf1=nf1),
        grid_spec=pltpu.PrefetchScalarGridSpec(
            num_scalar_prefetch=3,
            grid=(n_blk, nf1 + nf2),
            in_specs=in_specs,
            out_specs=pl.BlockSpec((rows, tn), lambda b, j, be, bv, na: (b, jnp.maximum(j - nf1, 0))),
            scratch_shapes=[pltpu.VMEM((rows * (d // LANES + 1), LANES), F32), pltpu.VMEM((rows, d), BF16),
                            pltpu.VMEM((nf1, rows, tf), BF16), pltpu.SemaphoreType.DMA(())]),
        out_shape=jax.ShapeDtypeStruct((n_blk * rows, d), F32),
        compiler_params=_cparams(58, 2),
        name="moe_experts",
    )(blk_e, blk_valid, n_active, src_tok, src_tok, h, w1, w1, b1.reshape(n_exp, 1, f2),
      b1.reshape(n_exp, 1, f2), w2, b2.reshape(n_exp, 1, d))


def _combine_kernel(dest_ref, nxt_ref, y_ref, x_ref, gate_ref, gt_ref, ng_ref, o_ref, buf_ref, sem,
                    *, final_norm):
    i = pl.program_id(0)
    tm = x_ref.shape[0]
    n_rows = tm * TOP_K
    slot = i % 2

    @pl.when(i == 0)
    def _():
        _row_gather(dest_ref, n_rows, y_ref, buf_ref.at[0], sem.at[0])

    @pl.when(i + 1 < pl.num_programs(0))
    def _():
        _row_gather(nxt_ref, n_rows, y_ref, buf_ref.at[1 - slot], sem.at[1 - slot])

    pltpu.make_async_copy(y_ref.at[pl.ds(0, n_rows)], buf_ref.at[slot], sem.at[slot]).wait()
    gate = gate_ref[...]
    acc = jnp.zeros(x_ref.shape, F32)
    for kk in range(TOP_K):
        acc = acc + gate[:, kk:kk + 1] * buf_ref[slot, pl.ds(kk * tm, tm), :]
    xn = x_ref[...] + gt_ref[0] * acc
    if final_norm:
        xn = xn * lax.rsqrt(jnp.mean(xn * xn, axis=-1, keepdims=True) + EPS) * ng_ref[...]
    o_ref[...] = xn


def combine(y_sorted, dest_km, xa, row0_tiles, gate, mods, gt_row, mod_idx, norm_g, final_norm):
    t, d = xa.shape
    tm = 256
    per = ROW_TILE // tm
    off = row0_tiles * per
    n = t // tm - off
    out_rows = n * tm if final_norm else t
    out_off = 0 if final_norm else off
    kernel = functools.partial(_combine_kernel, final_norm=final_norm)
    call = pl.pallas_call(
        kernel,
        grid=(n,),
        in_specs=[pl.BlockSpec((1, 1, TOP_K * tm), lambda i: (i, 0, 0), memory_space=pltpu.SMEM),
                  pl.BlockSpec((1, 1, TOP_K * tm), lambda i: (jnp.minimum(i + 1, n - 1), 0, 0),
                               memory_space=pltpu.SMEM),
                  pl.BlockSpec(memory_space=pl.ANY),
                  pl.BlockSpec((tm, d), lambda i: (i + off, 0)),
                  pl.BlockSpec((tm, LANES), lambda i: (i, 0)),
                  pl.BlockSpec((1, 1, d), lambda i: (gt_row + mod_idx((i + off) // per), 0, 0)),
                  pl.BlockSpec((1, d), lambda i: (0, 0))],
        out_specs=pl.BlockSpec((tm, d), lambda i: (i + out_off, 0)),
        out_shape=jax.ShapeDtypeStruct((out_rows, d), F32),
        scratch_shapes=[pltpu.VMEM((2, TOP_K * tm, d), F32), pltpu.SemaphoreType.DMA((2,))],
        compiler_params=_cparams(40, 1),
        name="moe_combine",
    )
    return call(dest_km, dest_km, y_sorted, xa, gate, mods, norm_g.reshape(1, d))


def _routing_tables(eidx, rank, counts, n_blk):
    t = eidx.shape[0]
    rows = EXPERT_ROWS
    padded = (counts + rows - 1) // rows * rows
    p_ends = jnp.cumsum(padded)
    p_starts = p_ends - padded
    starts = jnp.cumsum(counts) - counts
    dest = p_starts[eidx] + rank
    n_active = (p_ends[-1] // rows).astype(jnp.int32).reshape(1)
    blk_start = jnp.arange(n_blk, dtype=jnp.int32) * rows
    blk_e = jnp.minimum(jnp.searchsorted(p_ends, blk_start, side="right"), N_EXPERTS - 1).astype(jnp.int32)
    blk_valid = jnp.clip(counts[blk_e] - (blk_start - p_starts[blk_e]), 0, rows).astype(jnp.int32)
    order = jnp.argsort(eidx.reshape(-1), stable=True).astype(jnp.int32)
    tok_sorted = order // TOP_K
    within = (blk_start - p_starts[blk_e])[:, None] + jnp.arange(rows, dtype=jnp.int32)[None, :]
    pos = jnp.clip(starts[blk_e][:, None] + within, 0, t * TOP_K - 1)
    src = jnp.where(within < counts[blk_e][:, None], tok_sorted[pos], 0)
    return dest, src.reshape(n_blk, 1, rows).astype(jnp.int32), blk_e, blk_valid, n_active


def moe_ffn(xa, row0_tiles, gain, mods, base_row, mod_idx, wr_pad, br_pad, w1, b1, w2, b2, layer,
            norm_final, final_norm):
    n_b = 8
    h, eidx, gate, rank, cnt = router(xa, row0_tiles, gain, mods, base_row + 3 * n_b,
                                      base_row + 4 * n_b, mod_idx, wr_pad, br_pad)
    tl = eidx.shape[0]
    eidx4 = eidx[:, :TOP_K]
    rank4 = rank[:, :TOP_K]
    counts = cnt[0, :N_EXPERTS].astype(jnp.int32)
    n_blk = (tl * TOP_K + N_EXPERTS * (EXPERT_ROWS - 1)) // EXPERT_ROWS
    dest, src, blk_e, blk_valid, n_active = _routing_tables(eidx4, rank4, counts, n_blk)
    ys = experts(h, src * (w1.shape[2] // LANES), blk_e, blk_valid, n_active, w1, b1, w2, b2, layer)
    tmc = 256
    dest_km = dest.reshape(tl // tmc, tmc, TOP_K).transpose(0, 2, 1).reshape(tl // tmc, 1, TOP_K * tmc)
    return combine(ys, dest_km.astype(jnp.int32), xa, row0_tiles, gate, mods, base_row + 5 * n_b,
                   mod_idx, norm_final, final_norm)


def kernel(x, c, ctx, c_ctx, norm_mix, norm_ffn, w_ada, b_ada, w_in, conv_w, a_ws, a_bias, gla_wf, gla_bf, gla_wb, gla_bb, gla_norm, p_a, p_b, w_out, w_router, b_router, w1, b1, w2, b2, norm_final):
    n_b, seq, d = x.shape
    n_ctx = ctx.shape[1]
    depth = w_ada.shape[0]
    assert n_b * n_ctx == ROW_TILE and seq % ROW_TILE == 0 and n_b < 8
    tiles_per_batch = seq // ROW_TILE
    a_width = a_ws.shape[1] * a_ws.shape[2]
    dk = gla_wf.shape[2]
    dv = d

    def mod_idx(tile):
        return jnp.where(tile == 0, n_b, (tile - 1) // tiles_per_batch)

    xa = (ctx.reshape(n_b * n_ctx, d), x.reshape(n_b * seq, d))
    cond = jnp.zeros((8, d), F32).at[:n_b].set(c).at[n_b].set(c_ctx)
    mods_all = ada_mods(cond, w_ada, b_ada)
    mods = mods_all.reshape(depth, 8, N_MOD, d).transpose(0, 2, 1, 3).reshape(depth * N_MOD * 8, 1, d)

    c_u, c_q, c_r, c_lr, c_g = 0, 2 * a_width, 2 * a_width + 2 * dk + dv, 2 * a_width + 2 * dk + 2 * dv, \
        2 * a_width + 2 * dk + 2 * dv + 2 * GLA_RANK
    w_in_t = jnp.swapaxes(w_in, 1, 2)
    out = None
    for l in range(depth):
        last = l == depth - 1
        base = l * N_MOD * 8
        w_lr = jnp.zeros((1, LANES, d), F32).at[0, :2 * GLA_RANK].set(
            w_in_t[l, c_lr:c_lr + 2 * GLA_RANK])
        w_g = w_in_t[l:l + 1, c_g:]
        wf_pad = jnp.zeros((LANES, dk), F32).at[:GLA_RANK].set(gla_wf[l])
        wb_pad = jnp.zeros((LANES, dk), F32).at[GLA_RANK:2 * GLA_RANK].set(gla_wb[l])

        h = norm_mod(xa, norm_mix[l], mods, base + 0 * 8, base + 1 * 8, mod_idx)
        uv = matmul_act(h, w_in_t, l, c_u, 2 * a_width, act="gelu")
        qkv = matmul_act(h, w_in_t, l, c_q, 2 * dk + dv, act="conv", conv_w=conv_w[l],
                         ctx_seg=n_ctx, q_cols=dk, q_scale=float(dk // GLA_HEADS) ** -0.5)
        r_act = matmul_act(h, w_in_t, l, c_r, dv, act="silu")
        lr = matmul_act(h, w_lr, 0, 0, LANES, act="none", out_dtype=F32, tn=LANES)
        gates = matmul_act(h, w_g, 0, 0, 2 * d, act="sigmoid")

        n_ctx_blocks = n_b * n_ctx // GLA_BLOCK
        o_f = gla_scan(qkv, lr, None, None, wf_pad, gla_bf[l].reshape(1, dk), None,
                       n_b, n_ctx_blocks, reverse=False)
        y_gla = gla_scan(qkv, lr, r_act, o_f, wb_pad, gla_bb[l].reshape(1, dk),
                         gla_norm[l].reshape(1, -1), n_b, n_ctx_blocks, reverse=True)
        a = gmlp(uv, a_ws[l], a_bias[l].T)
        mrg = merge_proj(a, y_gla, gates, p_a, p_b, l)
        xa = out_proj(mrg, w_out, l, xa, mods, base + 2 * 8, mod_idx)

        wr_pad = jnp.zeros((d, LANES), F32).at[:, :N_EXPERTS].set(w_router[l])
        br_pad = jnp.full((1, LANES), -jnp.inf, F32).at[0, :N_EXPERTS].set(b_router[l])
        res = moe_ffn(xa, 1 if last else 0, norm_ffn[l], mods, base, mod_idx, wr_pad, br_pad,
                      w1, b1[l], w2, b2[l], l, norm_final, last)
        if last:
            out = res
        else:
            xa = res
    return out.reshape(n_b, seq, d)
```

```python
import functools

import jax
import jax.numpy as jnp
from jax import lax
from jax.experimental import pallas as pl
from jax.experimental.pallas import tpu as pltpu

F32 = jnp.float32
BF16 = jnp.bfloat16

GRID_W = 64
EPS = 1e-6
N_MOD = 6
A_GROUPS = 8
A_GROUP_W = 128
A_CHUNK = 128
GLA_HEADS = 4
GLA_RANK = 16
GLA_TAU = 16.0
GLA_CHUNK = 64
N_EXPERTS = 32
TOP_K = 4
SWIGLU_LIMIT = 7.0
SWIGLU_ALPHA = 1.702

LANES = 128
ROW_TILE = 1024
COL_TILE = 512
GLA_BLOCK = 256
GLA_HEADS_PER_STEP = 2
EXPERT_ROWS = 1024
EXPERT_SUB = 256
EXPERT_FT = 512
EXPERT_NT = 512
MIB = 1024 * 1024


def _cparams(vmem_mib, n_axes):
    return pltpu.CompilerParams(
        dimension_semantics=("arbitrary",) * n_axes,
        vmem_limit_bytes=int(vmem_mib * MIB))


def _sigmoid(t):
    return 0.5 * (1.0 + jnp.tanh(0.5 * t))


def _silu(t):
    return t * _sigmoid(t)


def _gelu_tanh(t):
    return 0.5 * t * (1.0 + jnp.tanh(0.7978845608028654 * (t + 0.044715 * (t * t * t))))


def _log_sigmoid(z):
    return jnp.minimum(z, 0.0) - jnp.log(1.0 + jnp.exp(-jnp.abs(z)))


def _ada_kernel(c_ref, w_ref, b_ref, o_ref):
    s = _silu(c_ref[...]).astype(BF16)
    w = w_ref[0].astype(BF16)
    o_ref[0] = jnp.dot(s, w, preferred_element_type=F32) + b_ref[0]


def ada_mods(cond, w_ada, b_ada):
    n_layer, d, n = w_ada.shape
    tn = 1024
    return pl.pallas_call(
        _ada_kernel,
        grid=(n_layer, n // tn),
        in_specs=[pl.BlockSpec((8, d), lambda l, j: (0, 0)),
                  pl.BlockSpec((1, d, tn), lambda l, j: (l, 0, j)),
                  pl.BlockSpec((1, 1, tn), lambda l, j: (l, 0, j))],
        out_specs=pl.BlockSpec((1, 8, tn), lambda l, j: (l, 0, j)),
        out_shape=jax.ShapeDtypeStruct((n_layer, 8, n), F32),
        compiler_params=_cparams(32, 2),
        name="ada_mods",
    )(cond, w_ada, b_ada.reshape(n_layer, 1, n))


def _stream_specs(stream, tm, tn, row_of, col_of):
    if not isinstance(stream, tuple):
        return [pl.BlockSpec((tm, tn), lambda *g: (row_of(*g), col_of(*g)))], [stream], 0
    ctx2d, lat2d = stream
    nct = ctx2d.shape[0] // tm
    specs = [pl.BlockSpec((tm, tn), lambda *g: (jnp.minimum(row_of(*g), nct - 1), col_of(*g))),
             pl.BlockSpec((tm, tn), lambda *g: (jnp.maximum(row_of(*g) - nct, 0), col_of(*g)))]
    return specs, [ctx2d, lat2d], nct


def _stream_tile(refs, tile, n_ctx_tiles):
    if len(refs) == 1:
        return refs[0][...]
    return jnp.where(tile < n_ctx_tiles, refs[0][...], refs[1][...])


def _norm_mod_kernel(*refs, n_ctx_tiles):
    g_ref, sh_ref, sc_ref, o_ref = refs[-4:]
    x = _stream_tile(refs[:-4], pl.program_id(0), n_ctx_tiles)
    r = lax.rsqrt(jnp.mean(x * x, axis=-1, keepdims=True) + EPS)
    h = (x * r) * g_ref[...]
    h = h * (1.0 + sc_ref[0]) + sh_ref[0]
    o_ref[...] = h.astype(o_ref.dtype)


def norm_mod(stream, gain, mods, sh_row, sc_row, mod_idx):
    t = sum(s.shape[0] for s in stream) if isinstance(stream, tuple) else stream.shape[0]
    d = gain.shape[0]
    tm = 512
    per = ROW_TILE // tm
    specs, arrays, nct = _stream_specs(stream, tm, d, lambda i: i, lambda i: 0)
    return pl.pallas_call(
        functools.partial(_norm_mod_kernel, n_ctx_tiles=nct),
        grid=(t // tm,),
        in_specs=specs + [
            pl.BlockSpec((1, d), lambda i: (0, 0)),
            pl.BlockSpec((1, 1, d), lambda i: (sh_row + mod_idx(i // per), 0, 0)),
            pl.BlockSpec((1, 1, d), lambda i: (sc_row + mod_idx(i // per), 0, 0))],
        out_specs=pl.BlockSpec((tm, d), lambda i: (i, 0)),
        out_shape=jax.ShapeDtypeStruct((t, d), BF16),
        compiler_params=_cparams(40, 1),
        name="norm_mod",
    )(*arrays, gain.reshape(1, d), mods, mods)


def _mm_kernel(*refs, act, ctx_seg, q_tiles, q_scale):
    if act == "conv":
        x_ref, w_ref, cw_ref, o_ref, wb_ref = refs
    else:
        x_ref, w_ref, o_ref, wb_ref = refs
    j = pl.program_id(0)
    i = pl.program_id(1)

    @pl.when(i == 0)
    def _():
        wb_ref[...] = w_ref[...].T.astype(BF16)

    y = jnp.dot(x_ref[...], wb_ref[...], preferred_element_type=F32)
    if act == "gelu":
        y = _gelu_tanh(y)
    elif act == "silu":
        y = _silu(y.astype(o_ref.dtype))
    elif act == "sigmoid":
        y = _sigmoid(y.astype(o_ref.dtype))
    elif act == "conv":
        seg = jnp.where(i == 0, ctx_seg, GRID_W)
        pos = lax.broadcasted_iota(jnp.int32, y.shape, 0) & (seg - 1)
        prev = jnp.where(pos == 0, 0.0, pltpu.roll(y, 1, 0))
        nxt = jnp.where(pos == seg - 1, 0.0, pltpu.roll(y, y.shape[0] - 1, 0))
        cw = cw_ref[...]
        y = _silu((cw[0:1] * prev + cw[1:2] * y + cw[2:3] * nxt).astype(o_ref.dtype))
        y = y * jnp.where(j < q_tiles, q_scale, 1.0).astype(o_ref.dtype)
    o_ref[...] = y.astype(o_ref.dtype)


def matmul_act(h, w_t, layer, col0, ncols, act="none", out_dtype=BF16, conv_w=None,
               ctx_seg=256, q_cols=0, q_scale=1.0, tn=COL_TILE):
    t, k = h.shape
    tm = ROW_TILE
    off = col0 // tn
    in_specs = [pl.BlockSpec((tm, k), lambda j, i: (i, 0)),
                pl.BlockSpec((None, tn, k), lambda j, i: (layer, j + off, 0))]
    args = [h, w_t]
    if act == "conv":
        in_specs.append(pl.BlockSpec((3, tn), lambda j, i: (0, j)))
        args.append(conv_w)
    return pl.pallas_call(
        functools.partial(_mm_kernel, act=act, ctx_seg=ctx_seg, q_tiles=q_cols // tn,
                          q_scale=q_scale),
        grid=(ncols // tn, t // tm),
        in_specs=in_specs,
        out_specs=pl.BlockSpec((tm, tn), lambda j, i: (i, j)),
        out_shape=jax.ShapeDtypeStruct((t, ncols), out_dtype),
        scratch_shapes=[pltpu.VMEM((k, tn), BF16)],
        compiler_params=_cparams(48, 2),
        name="matmul_" + act,
    )(*args)


def _gla_block(q, k, v, g, s_ref, tri, same, eye, reverse):
    c = GLA_CHUNK
    n_chunk = q.shape[0] // c
    cum = jnp.dot(tri, g, preferred_element_type=F32)
    tot = jnp.dot(same, g, preferred_element_type=F32)
    qt = (q * jnp.exp(cum)).astype(BF16)
    kt = (k * jnp.exp(-cum)).astype(BF16)
    ke = (k * jnp.exp(tot - cum)).astype(BF16)
    att = lax.dot_general(qt, kt, (((1,), (1,)), ((), ())), preferred_element_type=F32)
    att = jnp.where(tri > 0.0, att, 0.0).astype(BF16)
    o_intra = jnp.dot(att, v, preferred_element_type=F32)
    dk = g.shape[1]
    outs = [None] * n_chunk
    for ci in (range(n_chunk - 1, -1, -1) if reverse else range(n_chunk)):
        r0, r1 = ci * c, (ci + 1) * c
        s = s_ref[...]
        outs[ci] = o_intra[r0:r1] + jnp.dot(qt[r0:r1], s.astype(BF16), preferred_element_type=F32)
        tot_col = jnp.sum(jnp.where(eye, jnp.broadcast_to(tot[r0:r0 + 1], (dk, dk)), 0.0),
                          axis=1, keepdims=True)
        kv = lax.dot_general(ke[r0:r1], v[r0:r1], (((0,), (0,)), ((), ())),
                             preferred_element_type=F32)
        s_ref[...] = s * jnp.exp(tot_col) + kv
    return jnp.concatenate(outs, axis=0)


def _gla_kernel(*refs, reverse):
    if reverse:
        (q_ref, k_ref, v_ref, lr_ref, w_ref, b_ref, of_ref, r_ref, ng_ref, y_ref, s_ref) = refs
    else:
        (q_ref, k_ref, v_ref, lr_ref, w_ref, b_ref, o_ref, s_ref) = refs
    step = pl.program_id(2)

    @pl.when(step == 0)
    def _():
        s_ref[...] = jnp.zeros_like(s_ref)

    c = GLA_CHUNK
    n_head, dk, dv = s_ref.shape
    n_row = q_ref.shape[0]
    z = jnp.dot(lr_ref[...], w_ref[...], preferred_element_type=F32) + b_ref[...]
    g = _log_sigmoid(z) * (1.0 / GLA_TAU)
    row = lax.broadcasted_iota(jnp.int32, (n_row, n_row), 0)
    col = lax.broadcasted_iota(jnp.int32, (n_row, n_row), 1)
    shift = c.bit_length() - 1
    same_b = (row >> shift) == (col >> shift)
    same = same_b.astype(F32)
    tri = jnp.logical_and(same_b, (col >= row) if reverse else (col <= row)).astype(F32)
    eye = (lax.broadcasted_iota(jnp.int32, (dk, dk), 0)
           == lax.broadcasted_iota(jnp.int32, (dk, dk), 1))
    for hh in range(n_head):
        kc = pl.ds(hh * dk, dk)
        vc = pl.ds(hh * dv, dv)
        o = _gla_block(q_ref[:, kc].astype(F32), k_ref[:, kc].astype(F32), v_ref[:, vc],
                       g[:, hh * dk:(hh + 1) * dk], s_ref.at[hh], tri, same, eye, reverse)
        if reverse:
            o = o + of_ref[:, vc]
            o = o * lax.rsqrt(jnp.mean(o * o, axis=-1, keepdims=True) + EPS)
            y_ref[:, vc] = (o * ng_ref[...] * r_ref[:, vc].astype(F32)).astype(y_ref.dtype)
        else:
            o_ref[:, vc] = o


def gla_scan(qkv, lr, r_act, o_fwd, w_pad, bias, norm_g, n_batch, n_ctx_blocks, reverse):
    t = qkv.shape[0]
    dkh = w_pad.shape[1] // GLA_HEADS
    dv = qkv.shape[1] - 2 * w_pad.shape[1]
    dvh = dv // GLA_HEADS
    blk = GLA_BLOCK
    ctx_per_batch = n_ctx_blocks // n_batch
    assert ctx_per_batch == 1
    lat_per_batch = (t // blk - n_ctx_blocks) // n_batch
    n_step = ctx_per_batch + lat_per_batch

    def row_blk(b, s):
        if reverse:
            lat = n_ctx_blocks + lat_per_batch * b + (lat_per_batch - s)
        else:
            lat = n_ctx_blocks + lat_per_batch * b + (s - 1)
        return jnp.where(s == 0, b, lat)

    hp = GLA_HEADS_PER_STEP
    kw, vw = hp * dkh, hp * dvh
    k_off = w_pad.shape[1] // kw
    v_off = 2 * w_pad.shape[1] // vw
    in_specs = [pl.BlockSpec((blk, kw), lambda b, h, s: (row_blk(b, s), h)),
                pl.BlockSpec((blk, kw), lambda b, h, s: (row_blk(b, s), k_off + h)),
                pl.BlockSpec((blk, vw), lambda b, h, s: (row_blk(b, s), v_off + h)),
                pl.BlockSpec((blk, LANES), lambda b, h, s: (row_blk(b, s), 0)),
                pl.BlockSpec((LANES, kw), lambda b, h, s: (0, h)),
                pl.BlockSpec((1, kw), lambda b, h, s: (0, h))]
    args = [qkv, qkv, qkv, lr, w_pad, bias]
    if reverse:
        in_specs += [pl.BlockSpec((blk, vw), lambda b, h, s: (row_blk(b, s), h)),
                     pl.BlockSpec((blk, vw), lambda b, h, s: (row_blk(b, s), h)),
                     pl.BlockSpec((1, dvh), lambda b, h, s: (0, 0))]
        args += [o_fwd, r_act, norm_g]
        out_dtype = BF16
    else:
        out_dtype = F32
    return pl.pallas_call(
        functools.partial(_gla_kernel, reverse=reverse),
        grid=(n_batch, GLA_HEADS // hp, n_step),
        in_specs=in_specs,
        out_specs=pl.BlockSpec((blk, vw), lambda b, h, s: (row_blk(b, s), h)),
        out_shape=jax.ShapeDtypeStruct((t, dv), out_dtype),
        scratch_shapes=[pltpu.VMEM((hp, dkh, dvh), F32)],
        compiler_params=_cparams(32, 3),
        name="gla_bwd" if reverse else "gla_fwd",
    )(*args)


def _gmlp_kernel(u_ref, v_ref, ws_ref, bt_ref, o_ref):
    v = v_ref[...].astype(F32)
    mu = jnp.mean(v, axis=-1, keepdims=True)
    vc = v - mu
    var = jnp.mean(vc * vc, axis=-1, keepdims=True)
    vn = (vc * lax.rsqrt(var + EPS)).astype(BF16)
    n_chunk = v.shape[0] // A_CHUNK
    for gi in range(A_GROUPS):
        wsg = ws_ref[gi].astype(BF16)
        bias = bt_ref[:, gi:gi + 1]
        cols = slice(gi * A_GROUP_W, (gi + 1) * A_GROUP_W)
        for n in range(n_chunk):
            rows = slice(n * A_CHUNK, (n + 1) * A_CHUNK)
            s = jnp.dot(wsg, vn[rows, cols], preferred_element_type=F32) + bias
            o_ref[rows, cols] = (u_ref[rows, cols].astype(F32) * s).astype(o_ref.dtype)


def gmlp(uv, a_ws, a_bias_t):
    t = uv.shape[0]
    aw = uv.shape[1] // 2
    tm = 512
    return pl.pallas_call(
        _gmlp_kernel,
        grid=(t // tm,),
        in_specs=[pl.BlockSpec((tm, aw), lambda i: (i, 0)),
                  pl.BlockSpec((tm, aw), lambda i: (i, 1)),
                  pl.BlockSpec(a_ws.shape, lambda i: (0, 0, 0)),
                  pl.BlockSpec(a_bias_t.shape, lambda i: (0, 0))],
        out_specs=pl.BlockSpec((tm, aw), lambda i: (i, 0)),
        out_shape=jax.ShapeDtypeStruct((t, aw), BF16),
        compiler_params=_cparams(32, 1),
        name="gmlp",
    )(uv, uv, a_ws, a_bias_t)


def _merge_kernel(a_ref, y_ref, ga_ref, gb_ref, pa_ref, pb_ref, o_ref, pab_ref, pbb_ref):
    @pl.when(pl.program_id(1) == 0)
    def _():
        pab_ref[...] = pa_ref[...].astype(BF16)
        pbb_ref[...] = pb_ref[...].astype(BF16)

    ya = jnp.dot(a_ref[...], pab_ref[...], preferred_element_type=F32)
    yb = jnp.dot(y_ref[...], pbb_ref[...], preferred_element_type=F32)
    o_ref[...] = (ga_ref[...].astype(F32) * ya + gb_ref[...].astype(F32) * yb).astype(o_ref.dtype)


def merge_proj(a, yb, gates, p_a, p_b, layer):
    t, d = yb.shape
    tm, tn = ROW_TILE, COL_TILE
    nj = d // tn
    ka, kb = p_a.shape[1], p_b.shape[1]
    return pl.pallas_call(
        _merge_kernel,
        grid=(nj, t // tm),
        in_specs=[pl.BlockSpec((tm, ka), lambda j, i: (i, 0)),
                  pl.BlockSpec((tm, kb), lambda j, i: (i, 0)),
                  pl.BlockSpec((tm, tn), lambda j, i: (i, j)),
                  pl.BlockSpec((tm, tn), lambda j, i: (i, nj + j)),
                  pl.BlockSpec((None, ka, tn), lambda j, i: (layer, 0, j)),
                  pl.BlockSpec((None, kb, tn), lambda j, i: (layer, 0, j))],
        out_specs=pl.BlockSpec((tm, tn), lambda j, i: (i, j)),
        out_shape=jax.ShapeDtypeStruct((t, d), BF16),
        scratch_shapes=[pltpu.VMEM((ka, tn), BF16), pltpu.VMEM((kb, tn), BF16)],
        compiler_params=_cparams(48, 2),
        name="merge_proj",
    )(a, yb, gates, gates, p_a, p_b)


def _out_proj_kernel(*refs, n_ctx_tiles):
    m_ref, w_ref = refs[:2]
    gt_ref, o_ref, wb_ref = refs[-3:]
    i = pl.program_id(1)

    @pl.when(i == 0)
    def _():
        wb_ref[...] = w_ref[...].astype(BF16)

    y = jnp.dot(m_ref[...], wb_ref[...], preferred_element_type=F32)
    o_ref[...] = _stream_tile(refs[2:-3], i, n_ctx_tiles) + gt_ref[0] * y


def out_proj(mrg, w_out, layer, stream, mods, gt_row, mod_idx):
    t, d = mrg.shape
    tm, tn = ROW_TILE, COL_TILE
    specs, arrays, nct = _stream_specs(stream, tm, tn, lambda j, i: i, lambda j, i: j)
    return pl.pallas_call(
        functools.partial(_out_proj_kernel, n_ctx_tiles=nct),
        grid=(d // tn, t // tm),
        in_specs=[pl.BlockSpec((tm, d), lambda j, i: (i, 0)),
                  pl.BlockSpec((None, d, tn), lambda j, i: (layer, 0, j))] + specs + [
                  pl.BlockSpec((1, 1, tn), lambda j, i: (gt_row + mod_idx(i), 0, j))],
        out_specs=pl.BlockSpec((tm, tn), lambda j, i: (i, j)),
        out_shape=jax.ShapeDtypeStruct((t, d), F32),
        scratch_shapes=[pltpu.VMEM((d, tn), BF16)],
        compiler_params=_cparams(48, 2),
        name="out_proj",
    )(mrg, w_out, *arrays, mods)


def _router_kernel(x_ref, g_ref, sh_ref, sc_ref, wr_ref, br_ref,
                   h_ref, idx_ref, gate_ref, rank_ref, cnt_ref, carry_ref):
    @pl.when(pl.program_id(0) == 0)
    def _():
        carry_ref[...] = jnp.zeros_like(carry_ref)

    x = x_ref[...]
    r = lax.rsqrt(jnp.mean(x * x, axis=-1, keepdims=True) + EPS)
    h = (x * r) * g_ref[...]
    h = h * (1.0 + sc_ref[0]) + sh_ref[0]
    n_piece = h.shape[1] // LANES
    for p in range(n_piece):
        h_ref[pl.ds(p, h.shape[0], stride=n_piece), :] = h[:, p * LANES:(p + 1) * LANES]
    logits = jnp.dot(h, wr_ref[...], preferred_element_type=F32) + br_ref[...]
    tm = x.shape[0]
    lane = lax.broadcasted_iota(jnp.int32, (tm, LANES), 1)
    lane_f = lane.astype(F32)
    member = jnp.zeros((tm, LANES), F32)
    vals, sels = [], []
    idx_out = jnp.zeros((tm, LANES), F32)
    for kk in range(TOP_K):
        m = jnp.max(logits, axis=1, keepdims=True)
        ik = jnp.min(jnp.where(logits == m, lane_f, float(LANES)), axis=1, keepdims=True)
        sel = lane_f == ik
        logits = jnp.where(sel, -jnp.inf, logits)
        member = member + sel.astype(F32)
        idx_out = jnp.where(lane == kk, ik, idx_out)
        vals.append(m)
        sels.append(sel)
    es = [jnp.exp(v - vals[0]) for v in vals]
    denom = es[0] + es[1] + es[2] + es[3]
    gate_out = jnp.zeros((tm, LANES), F32)
    for kk in range(TOP_K):
        gate_out = jnp.where(lane == kk, es[kk] / denom, gate_out)
    rr = lax.broadcasted_iota(jnp.int32, (tm, tm), 0)
    cc = lax.broadcasted_iota(jnp.int32, (tm, tm), 1)
    strict = (cc < rr).astype(BF16)
    before = jnp.dot(strict, member.astype(BF16), preferred_element_type=F32) + carry_ref[...]
    rank_out = jnp.zeros((tm, LANES), F32)
    for kk in range(TOP_K):
        rk = jnp.sum(jnp.where(sels[kk], before, 0.0), axis=1, keepdims=True)
        rank_out = jnp.where(lane == kk, rk, rank_out)
    carry_ref[...] = carry_ref[...] + jnp.sum(member, axis=0, keepdims=True)
    idx_ref[...] = idx_out.astype(jnp.int32)
    gate_ref[...] = gate_out
    rank_ref[...] = rank_out.astype(jnp.int32)
    cnt_ref[...] = carry_ref[...]


def router(xa, row0_tiles, gain, mods, sh_row, sc_row, mod_idx, wr_pad, br_pad):
    t, d = xa.shape
    tm = 512
    per = ROW_TILE // tm
    off = row0_tiles * per
    n = t // tm - off
    tl = n * tm
    small = lambda dt: jax.ShapeDtypeStruct((tl, LANES), dt)
    return pl.pallas_call(
        _router_kernel,
        grid=(n,),
        in_specs=[pl.BlockSpec((tm, d), lambda i: (i + off, 0)),
                  pl.BlockSpec((1, d), lambda i: (0, 0)),
                  pl.BlockSpec((1, 1, d), lambda i: (sh_row + mod_idx((i + off) // per), 0, 0)),
                  pl.BlockSpec((1, 1, d), lambda i: (sc_row + mod_idx((i + off) // per), 0, 0)),
                  pl.BlockSpec((d, LANES), lambda i: (0, 0)),
                  pl.BlockSpec((1, LANES), lambda i: (0, 0))],
        out_specs=[pl.BlockSpec((tm * (d // LANES), LANES), lambda i: (i, 0)),
                   pl.BlockSpec((tm, LANES), lambda i: (i, 0)),
                   pl.BlockSpec((tm, LANES), lambda i: (i, 0)),
                   pl.BlockSpec((tm, LANES), lambda i: (i, 0)),
                   pl.BlockSpec((1, LANES), lambda i: (0, 0))],
        out_shape=[jax.ShapeDtypeStruct((tl * (d // LANES), LANES), F32), small(jnp.int32), small(F32),
                   small(jnp.int32), jax.ShapeDtypeStruct((1, LANES), F32)],
        scratch_shapes=[pltpu.VMEM((1, LANES), F32)],
        compiler_params=_cparams(40, 1),
        name="router",
    )(xa, gain.reshape(1, d), mods, mods, wr_pad, br_pad)


DMA_UNROLL = 8


def _row_gather(idx_ref, n_rows, src_ref, dst_ref, sem, run=1, priority=0):
    pitch = dst_ref.shape[0] // n_rows

    def body(g, carry):
        for u in range(DMA_UNROLL):
            r = g * DMA_UNROLL + u
            first = idx_ref[0, 0, r] if run == 1 else pl.multiple_of(idx_ref[0, 0, r], run)
            pltpu.async_copy(src_ref.at[pl.ds(first, run)], dst_ref.at[pl.ds(r * pitch, run)], sem,
                             priority=priority)
        return carry
    lax.fori_loop(0, n_rows // DMA_UNROLL, body, 0)


def _expert_kernel(be_ref, bv_ref, na_ref, cur_ref, nxt_ref, h_ref, w1g_ref, w1l_ref, b1_ref,
                   w2_ref, b2_ref, o_ref, gbuf, xs, act_s, sem, *, nf1):
    b = pl.program_id(0)
    j = pl.program_id(1)
    n_act = na_ref[0]
    active = b < n_act
    rows = EXPERT_ROWS
    n_sub = (bv_ref[b] + EXPERT_SUB - 1) // EXPERT_SUB
    jj = jnp.minimum(j, nf1 - 1)

    n_piece = xs.shape[1] // LANES
    pitch = gbuf.shape[0] // rows
    e = be_ref[jnp.minimum(b, n_act - 1)]
    tf, tn = act_s.shape[2], o_ref.shape[1]
    b1g = b1_ref[e, :, pl.ds(pl.multiple_of(jj * tf, tf), tf)]
    b1l = b1_ref[e, :, pl.ds(pl.multiple_of((nf1 + jj) * tf, tf), tf)]
    b2 = b2_ref[e, :, pl.ds(pl.multiple_of(jnp.maximum(j - nf1, 0) * tn, tn), tn)]

    @pl.when(jnp.logical_and(active, j == 0))
    def _():
        @pl.when(b == 0)
        def _():
            _row_gather(cur_ref, rows, h_ref, gbuf, sem, run=n_piece)

        pltpu.make_async_copy(h_ref.at[pl.ds(0, rows * n_piece)],
                              gbuf.at[pl.ds(0, rows * n_piece)], sem).wait()
        for p in range(n_piece):
            xs[:, p * LANES:(p + 1) * LANES] = gbuf[pl.ds(p, rows, stride=pitch), :].astype(BF16)

        @pl.when(b + 1 < n_act)
        def _():
            _row_gather(nxt_ref, rows, h_ref, gbuf, sem, run=n_piece, priority=1)

    @pl.when(jnp.logical_and(jnp.logical_not(active), j >= nf1))
    def _():
        o_ref[...] = jnp.zeros_like(o_ref)

    for m in range(1, rows // EXPERT_SUB + 1):
        r = m * EXPERT_SUB

        @pl.when(jnp.logical_and(active, jnp.logical_and(j < nf1, n_sub == m)))
        def _():
            x = xs[0:r, :]
            hg = jnp.dot(x, w1g_ref[0].astype(BF16), preferred_element_type=F32) + b1g
            hl = jnp.dot(x, w1l_ref[0].astype(BF16), preferred_element_type=F32) + b1l
            hg = jnp.minimum(hg, SWIGLU_LIMIT)
            hl = jnp.clip(hl, -SWIGLU_LIMIT, SWIGLU_LIMIT)
            act = hg * _sigmoid(SWIGLU_ALPHA * hg) * (hl + 1.0)
            act_s[jj, 0:r, :] = act.astype(BF16)

        @pl.when(jnp.logical_and(active, jnp.logical_and(j >= nf1, n_sub == m)))
        def _():
            a = jnp.concatenate([act_s[t, 0:r, :] for t in range(nf1)], axis=1)
            o_ref[0:r, :] = jnp.dot(a, w2_ref[0].astype(BF16),
                                    preferred_element_type=F32) + b2
            if r < rows:
                o_ref[r:rows, :] = jnp.zeros((rows - r, o_ref.shape[1]), o_ref.dtype)


def experts(h, src_tok, blk_e, blk_valid, n_active, w1, b1, w2, b2, layer):
    _, n_exp, d, f2 = w1.shape
    f = f2 // 2
    tf, tn = EXPERT_FT, EXPERT_NT
    nf1, nf2 = f // tf, d // tn
    n_blk = blk_e.shape[0]
    rows = EXPERT_ROWS

    def eb(b, be, na):
        return be[jnp.minimum(b, na[0] - 1)]

    def j1(b, j, na):
        return jnp.where(b < na[0], jnp.minimum(j, nf1 - 1), nf1 - 1)

    def w2_idx(b, j, be, na):
        cur = jnp.logical_and(b < na[0], j >= nf1)
        e_prev = be[jnp.maximum(jnp.minimum(b, na[0]) - 1, 0)]
        return (layer, jnp.where(cur, eb(b, be, na), e_prev), 0, jnp.where(cur, j - nf1, nf2 - 1))

    smem_rows = functools.partial(pl.BlockSpec, (1, 1, rows), memory_space=pltpu.SMEM)
    in_specs = [
        smem_rows(lambda b, j, be, bv, na: (b, 0, 0)),
        smem_rows(lambda b, j, be, bv, na: (jnp.minimum(b + 1, n_blk - 1), 0, 0)),
        pl.BlockSpec(memory_space=pl.ANY),
        pl.BlockSpec((None, 1, d, tf), lambda b, j, be, bv, na: (layer, eb(b, be, na), 0, j1(b, j, na))),
        pl.BlockSpec((None, 1, d, tf), lambda b, j, be, bv, na: (layer, eb(b, be, na), 0, nf1 + j1(b, j, na))),
        pl.BlockSpec((n_exp, 1, f2), lambda b, j, be, bv, na: (0, 0, 0)),
        pl.BlockSpec((None, 1, f, tn), lambda b, j, be, bv, na: w2_idx(b, j, be, na)),
        pl.BlockSpec((n_exp, 1, d), lambda b, j, be, bv, na: (0, 0, 0)),
    ]
    return pl.pallas_call(
        functools.partial(_expert_kernel, nf1=nf1),
        grid_spec=pltpu.PrefetchScalarGridSpec(
            num_scalar_prefetch=3,
            grid=(n_blk, nf1 + nf2),
            in_specs=in_specs,
            out_specs=pl.BlockSpec((rows, tn), lambda b, j, be, bv, na: (b, jnp.maximum(j - nf1, 0))),
            scratch_shapes=[pltpu.VMEM((rows * (d // LANES + 1), LANES), F32), pltpu.VMEM((rows, d), BF16),
                            pltpu.VMEM((nf1, rows, tf), BF16), pltpu.SemaphoreType.DMA(())]),
        out_shape=jax.ShapeDtypeStruct((n_blk * rows, d), F32),
        compiler_params=_cparams(58, 2),
        name="moe_experts",
    )(blk_e, blk_valid, n_active, src_tok, src_tok, h, w1, w1, b1.reshape(n_exp, 1, f2),
      w2, b2.reshape(n_exp, 1, d))


def _combine_kernel(dest_ref, nxt_ref, y_ref, x_ref, gate_ref, gt_ref, ng_ref, o_ref, buf_ref, sem,
                    *, final_norm):
    i = pl.program_id(0)
    tm = x_ref.shape[0]
    n_rows = tm * TOP_K
    slot = i % 2

    @pl.when(i == 0)
    def _():
        _row_gather(dest_ref, n_rows, y_ref, buf_ref.at[0], sem.at[0])

    @pl.when(i + 1 < pl.num_programs(0))
    def _():
        _row_gather(nxt_ref, n_rows, y_ref, buf_ref.at[1 - slot], sem.at[1 - slot])

    pltpu.make_async_copy(y_ref.at[pl.ds(0, n_rows)], buf_ref.at[slot], sem.at[slot]).wait()
    gate = gate_ref[...]
    acc = jnp.zeros(x_ref.shape, F32)
    for kk in range(TOP_K):
        acc = acc + gate[:, kk:kk + 1] * buf_ref[slot, pl.ds(kk * tm, tm), :]
    xn = x_ref[...] + gt_ref[0] * acc
    if final_norm:
        xn = xn * lax.rsqrt(jnp.mean(xn * xn, axis=-1, keepdims=True) + EPS) * ng_ref[...]
    o_ref[...] = xn


def combine(y_sorted, dest_km, xa, row0_tiles, gate, mods, gt_row, mod_idx, norm_g, final_norm):
    t, d = xa.shape
    tm = 256
    per = ROW_TILE // tm
    off = row0_tiles * per
    n = t // tm - off
    out_rows = n * tm if final_norm else t
    out_off = 0 if final_norm else off
    kernel = functools.partial(_combine_kernel, final_norm=final_norm)
    call = pl.pallas_call(
        kernel,
        grid=(n,),
        in_specs=[pl.BlockSpec((1, 1, TOP_K * tm), lambda i: (i, 0, 0), memory_space=pltpu.SMEM),
                  pl.BlockSpec((1, 1, TOP_K * tm), lambda i: (jnp.minimum(i + 1, n - 1), 0, 0),
                               memory_space=pltpu.SMEM),
                  pl.BlockSpec(memory_space=pl.ANY),
                  pl.BlockSpec((tm, d), lambda i: (i + off, 0)),
                  pl.BlockSpec((tm, LANES), lambda i: (i, 0)),
                  pl.BlockSpec((1, 1, d), lambda i: (gt_row + mod_idx((i + off) // per), 0, 0)),
                  pl.BlockSpec((1, d), lambda i: (0, 0))],
        out_specs=pl.BlockSpec((tm, d), lambda i: (i + out_off, 0)),
        out_shape=jax.ShapeDtypeStruct((out_rows, d), F32),
        scratch_shapes=[pltpu.VMEM((2, TOP_K * tm, d), F32), pltpu.SemaphoreType.DMA((2,))],
        compiler_params=_cparams(40, 1),
        name="moe_combine",
    )
    return call(dest_km, dest_km, y_sorted, xa, gate, mods, norm_g.reshape(1, d))


def _routing_tables(eidx, rank, counts, n_blk):
    t = eidx.shape[1]
    rows = EXPERT_ROWS
    padded = (counts + rows - 1) // rows * rows
    p_ends = jnp.cumsum(padded)
    p_starts = p_ends - padded
    starts = jnp.cumsum(counts) - counts
    dest = p_starts[eidx] + rank
    n_active = (p_ends[-1] // rows).astype(jnp.int32).reshape(1)
    blk_start = jnp.arange(n_blk, dtype=jnp.int32) * rows
    blk_e = jnp.minimum(jnp.searchsorted(p_ends, blk_start, side="right"), N_EXPERTS - 1).astype(jnp.int32)
    blk_valid = jnp.clip(counts[blk_e] - (blk_start - p_starts[blk_e]), 0, rows).astype(jnp.int32)
    tok_bits = max(t - 1, 1).bit_length()
    keys = (eidx << tok_bits) | jnp.arange(t, dtype=jnp.int32)[None, :]
    tok_sorted = jnp.sort(keys.reshape(-1)) & ((1 << tok_bits) - 1)
    within = (blk_start - p_starts[blk_e])[:, None] + jnp.arange(rows, dtype=jnp.int32)[None, :]
    pos = jnp.clip(starts[blk_e][:, None] + within, 0, t * TOP_K - 1)
    src = jnp.where(within < counts[blk_e][:, None], tok_sorted[pos], 0)
    return dest, src.reshape(n_blk, 1, rows).astype(jnp.int32), blk_e, blk_valid, n_active


def moe_ffn(xa, row0_tiles, gain, mods, base_row, mod_idx, wr_pad, br_pad, w1, b1, w2, b2, layer,
            norm_final, final_norm):
    n_b = 8
    h, eidx, gate, rank, cnt = router(xa, row0_tiles, gain, mods, base_row + 3 * n_b,
                                      base_row + 4 * n_b, mod_idx, wr_pad, br_pad)
    tl = eidx.shape[0]
    eidx4 = eidx[:, :TOP_K].T
    rank4 = rank[:, :TOP_K].T
    counts = cnt[0, :N_EXPERTS].astype(jnp.int32)
    n_blk = (tl * TOP_K + N_EXPERTS * (EXPERT_ROWS - 1)) // EXPERT_ROWS
    dest, src, blk_e, blk_valid, n_active = _routing_tables(eidx4, rank4, counts, n_blk)
    ys = experts(h, src * (w1.shape[2] // LANES), blk_e, blk_valid, n_active, w1, b1, w2, b2, layer)
    tmc = 256
    dest_km = dest.reshape(TOP_K, tl // tmc, tmc).transpose(1, 0, 2).reshape(tl // tmc, 1, TOP_K * tmc)
    return combine(ys, dest_km.astype(jnp.int32), xa, row0_tiles, gate, mods, base_row + 5 * n_b,
                   mod_idx, norm_final, final_norm)


def kernel(x, c, ctx, c_ctx, norm_mix, norm_ffn, w_ada, b_ada, w_in, conv_w, a_ws, a_bias, gla_wf, gla_bf, gla_wb, gla_bb, gla_norm, p_a, p_b, w_out, w_router, b_router, w1, b1, w2, b2, norm_final):
    n_b, seq, d = x.shape
    n_ctx = ctx.shape[1]
    depth = w_ada.shape[0]
    assert n_b * n_ctx == ROW_TILE and seq % ROW_TILE == 0 and n_b < 8
    tiles_per_batch = seq // ROW_TILE
    a_width = a_ws.shape[1] * a_ws.shape[2]
    dk = gla_wf.shape[2]
    dv = d

    def mod_idx(tile):
        return jnp.where(tile == 0, n_b, (tile - 1) // tiles_per_batch)

    xa = (ctx.reshape(n_b * n_ctx, d), x.reshape(n_b * seq, d))
    cond = jnp.zeros((8, d), F32).at[:n_b].set(c).at[n_b].set(c_ctx)
    mods_all = ada_mods(cond, w_ada, b_ada)
    mods = mods_all.reshape(depth, 8, N_MOD, d).transpose(0, 2, 1, 3).reshape(depth * N_MOD * 8, 1, d)

    c_u, c_q, c_r, c_lr, c_g = 0, 2 * a_width, 2 * a_width + 2 * dk + dv, 2 * a_width + 2 * dk + 2 * dv, \
        2 * a_width + 2 * dk + 2 * dv + 2 * GLA_RANK
    w_in_t = jnp.swapaxes(w_in, 1, 2)
    out = None
    for l in range(depth):
        last = l == depth - 1
        base = l * N_MOD * 8
        w_lr = jnp.zeros((1, LANES, d), F32).at[0, :2 * GLA_RANK].set(
            w_in_t[l, c_lr:c_lr + 2 * GLA_RANK])
        w_g = w_in_t[l:l + 1, c_g:]
        wf_pad = jnp.zeros((LANES, dk), F32).at[:GLA_RANK].set(gla_wf[l])
        wb_pad = jnp.zeros((LANES, dk), F32).at[GLA_RANK:2 * GLA_RANK].set(gla_wb[l])

        h = norm_mod(xa, norm_mix[l], mods, base + 0 * 8, base + 1 * 8, mod_idx)
        uv = matmul_act(h, w_in_t, l, c_u, 2 * a_width, act="gelu")
        qkv = matmul_act(h, w_in_t, l, c_q, 2 * dk + dv, act="conv", conv_w=conv_w[l],
                         ctx_seg=n_ctx, q_cols=dk, q_scale=float(dk // GLA_HEADS) ** -0.5)
        r_act = matmul_act(h, w_in_t, l, c_r, dv, act="silu")
        lr = matmul_act(h, w_lr, 0, 0, LANES, act="none", out_dtype=F32, tn=LANES)
        gates = matmul_act(h, w_g, 0, 0, 2 * d, act="sigmoid")

        n_ctx_blocks = n_b * n_ctx // GLA_BLOCK
        o_f = gla_scan(qkv, lr, None, None, wf_pad, gla_bf[l].reshape(1, dk), None,
                       n_b, n_ctx_blocks, reverse=False)
        y_gla = gla_scan(qkv, lr, r_act, o_f, wb_pad, gla_bb[l].reshape(1, dk),
                         gla_norm[l].reshape(1, -1), n_b, n_ctx_blocks, reverse=True)
        a = gmlp(uv, a_ws[l], a_bias[l].T)
        mrg = merge_proj(a, y_gla, gates, p_a, p_b, l)
        xa = out_proj(mrg, w_out, l, xa, mods, base + 2 * 8, mod_idx)

        wr_pad = jnp.zeros((d, LANES), F32).at[:, :N_EXPERTS].set(w_router[l])
        br_pad = jnp.full((1, LANES), -jnp.inf, F32).at[0, :N_EXPERTS].set(b_router[l])
        res = moe_ffn(xa, 1 if last else 0, norm_ffn[l], mods, base, mod_idx, wr_pad, br_pad,
                      w1, b1[l], w2, b2[l], l, norm_final, last)
        if last:
            out = res
        else:
            xa = res
    return out.reshape(n_b, seq, d)
```

```python
import functools

import jax
import jax.numpy as jnp
from jax import lax
from jax.experimental import pallas as pl
from jax.experimental.pallas import tpu as pltpu

F32 = jnp.float32
BF16 = jnp.bfloat16

GRID_W = 64
EPS = 1e-6
N_MOD = 6
A_GROUPS = 8
A_GROUP_W = 128
A_CHUNK = 128
GLA_HEADS = 4
GLA_RANK = 16
GLA_TAU = 16.0
GLA_CHUNK = 64
N_EXPERTS = 32
TOP_K = 4
SWIGLU_LIMIT = 7.0
SWIGLU_ALPHA = 1.702

LANES = 128
ROW_TILE = 1024
COL_TILE = 512
GLA_BLOCK = 256
GLA_HEADS_PER_STEP = 2
EXPERT_ROWS = 1024
EXPERT_SUB = 256
EXPERT_FT = 512
EXPERT_NT = 512
MIB = 1024 * 1024


def _cparams(vmem_mib, n_axes):
    return pltpu.CompilerParams(
        dimension_semantics=("arbitrary",) * n_axes,
        vmem_limit_bytes=int(vmem_mib * MIB))


def _sigmoid(t):
    return 0.5 * (1.0 + jnp.tanh(0.5 * t))


def _silu(t):
    return t * _sigmoid(t)


def _gelu_tanh(t):
    return 0.5 * t * (1.0 + jnp.tanh(0.7978845608028654 * (t + 0.044715 * (t * t * t))))


def _log_sigmoid(z):
    return jnp.minimum(z, 0.0) - jnp.log(1.0 + jnp.exp(-jnp.abs(z)))


def _ada_kernel(c_ref, w_ref, b_ref, o_ref):
    s = _silu(c_ref[...]).astype(BF16)
    w = w_ref[0].astype(BF16)
    o_ref[0] = jnp.dot(s, w, preferred_element_type=F32) + b_ref[0]


def ada_mods(cond, w_ada, b_ada):
    n_layer, d, n = w_ada.shape
    tn = 1024
    return pl.pallas_call(
        _ada_kernel,
        grid=(n_layer, n // tn),
        in_specs=[pl.BlockSpec((8, d), lambda l, j: (0, 0)),
                  pl.BlockSpec((1, d, tn), lambda l, j: (l, 0, j)),
                  pl.BlockSpec((1, 1, tn), lambda l, j: (l, 0, j))],
        out_specs=pl.BlockSpec((1, 8, tn), lambda l, j: (l, 0, j)),
        out_shape=jax.ShapeDtypeStruct((n_layer, 8, n), F32),
        compiler_params=_cparams(32, 2),
        name="ada_mods",
    )(cond, w_ada, b_ada.reshape(n_layer, 1, n))


def _stream_specs(stream, tm, tn, row_of, col_of):
    if not isinstance(stream, tuple):
        return [pl.BlockSpec((tm, tn), lambda *g: (row_of(*g), col_of(*g)))], [stream], 0
    ctx2d, lat2d = stream
    nct = ctx2d.shape[0] // tm
    specs = [pl.BlockSpec((tm, tn), lambda *g: (jnp.minimum(row_of(*g), nct - 1), col_of(*g))),
             pl.BlockSpec((tm, tn), lambda *g: (jnp.maximum(row_of(*g) - nct, 0), col_of(*g)))]
    return specs, [ctx2d, lat2d], nct


def _stream_tile(refs, tile, n_ctx_tiles):
    if len(refs) == 1:
        return refs[0][...]
    return jnp.where(tile < n_ctx_tiles, refs[0][...], refs[1][...])


def _norm_mod_kernel(*refs, n_ctx_tiles):
    g_ref, sh_ref, sc_ref, o_ref = refs[-4:]
    x = _stream_tile(refs[:-4], pl.program_id(0), n_ctx_tiles)
    r = lax.rsqrt(jnp.mean(x * x, axis=-1, keepdims=True) + EPS)
    h = (x * r) * g_ref[...]
    h = h * (1.0 + sc_ref[0]) + sh_ref[0]
    o_ref[...] = h.astype(o_ref.dtype)


def norm_mod(stream, gain, mods, sh_row, sc_row, mod_idx):
    t = sum(s.shape[0] for s in stream) if isinstance(stream, tuple) else stream.shape[0]
    d = gain.shape[0]
    tm = 512
    per = ROW_TILE // tm
    specs, arrays, nct = _stream_specs(stream, tm, d, lambda i: i, lambda i: 0)
    return pl.pallas_call(
        functools.partial(_norm_mod_kernel, n_ctx_tiles=nct),
        grid=(t // tm,),
        in_specs=specs + [
            pl.BlockSpec((1, d), lambda i: (0, 0)),
            pl.BlockSpec((1, 1, d), lambda i: (sh_row + mod_idx(i // per), 0, 0)),
            pl.BlockSpec((1, 1, d), lambda i: (sc_row + mod_idx(i // per), 0, 0))],
        out_specs=pl.BlockSpec((tm, d), lambda i: (i, 0)),
        out_shape=jax.ShapeDtypeStruct((t, d), BF16),
        compiler_params=_cparams(40, 1),
        name="norm_mod",
    )(*arrays, gain.reshape(1, d), mods, mods)


def _mm_kernel(*refs, act, ctx_seg, q_tiles, q_scale):
    if act == "conv":
        x_ref, w_ref, cw_ref, o_ref, wb_ref = refs
    else:
        x_ref, w_ref, o_ref, wb_ref = refs
    j = pl.program_id(0)
    i = pl.program_id(1)

    @pl.when(i == 0)
    def _():
        wb_ref[...] = w_ref[...].T.astype(BF16)

    y = jnp.dot(x_ref[...], wb_ref[...], preferred_element_type=F32)
    if act == "gelu":
        y = _gelu_tanh(y)
    elif act == "silu":
        y = _silu(y.astype(o_ref.dtype))
    elif act == "sigmoid":
        y = _sigmoid(y.astype(o_ref.dtype))
    elif act == "conv":
        seg = jnp.where(i == 0, ctx_seg, GRID_W)
        pos = lax.broadcasted_iota(jnp.int32, y.shape, 0) & (seg - 1)
        prev = jnp.where(pos == 0, 0.0, pltpu.roll(y, 1, 0))
        nxt = jnp.where(pos == seg - 1, 0.0, pltpu.roll(y, y.shape[0] - 1, 0))
        cw = cw_ref[...]
        y = _silu((cw[0:1] * prev + cw[1:2] * y + cw[2:3] * nxt).astype(o_ref.dtype))
        y = y * jnp.where(j < q_tiles, q_scale, 1.0).astype(o_ref.dtype)
    o_ref[...] = y.astype(o_ref.dtype)


def matmul_act(h, w_t, layer, col0, ncols, act="none", out_dtype=BF16, conv_w=None,
               ctx_seg=256, q_cols=0, q_scale=1.0, tn=COL_TILE):
    t, k = h.shape
    tm = ROW_TILE
    off = col0 // tn
    in_specs = [pl.BlockSpec((tm, k), lambda j, i: (i, 0)),
                pl.BlockSpec((None, tn, k), lambda j, i: (layer, j + off, 0))]
    args = [h, w_t]
    if act == "conv":
        in_specs.append(pl.BlockSpec((3, tn), lambda j, i: (0, j)))
        args.append(conv_w)
    return pl.pallas_call(
        functools.partial(_mm_kernel, act=act, ctx_seg=ctx_seg, q_tiles=q_cols // tn,
                          q_scale=q_scale),
        grid=(ncols // tn, t // tm),
        in_specs=in_specs,
        out_specs=pl.BlockSpec((tm, tn), lambda j, i: (i, j)),
        out_shape=jax.ShapeDtypeStruct((t, ncols), out_dtype),
        scratch_shapes=[pltpu.VMEM((k, tn), BF16)],
        compiler_params=_cparams(48, 2),
        name="matmul_" + act,
    )(*args)


def _gla_block(q, k, v, g, s_ref, tri, same, eye, reverse):
    c = GLA_CHUNK
    n_chunk = q.shape[0] // c
    cum = jnp.dot(tri, g, preferred_element_type=F32)
    tot = jnp.dot(same, g, preferred_element_type=F32)
    qt = (q * jnp.exp(cum)).astype(BF16)
    kt = (k * jnp.exp(-cum)).astype(BF16)
    ke = (k * jnp.exp(tot - cum)).astype(BF16)
    att = lax.dot_general(qt, kt, (((1,), (1,)), ((), ())), preferred_element_type=F32)
    att = jnp.where(tri > 0.0, att, 0.0).astype(BF16)
    o_intra = jnp.dot(att, v, preferred_element_type=F32)
    dk = g.shape[1]
    outs = [None] * n_chunk
    for ci in (range(n_chunk - 1, -1, -1) if reverse else range(n_chunk)):
        r0, r1 = ci * c, (ci + 1) * c
        s = s_ref[...]
        outs[ci] = o_intra[r0:r1] + jnp.dot(qt[r0:r1], s.astype(BF16), preferred_element_type=F32)
        tot_col = jnp.sum(jnp.where(eye, jnp.broadcast_to(tot[r0:r0 + 1], (dk, dk)), 0.0),
                          axis=1, keepdims=True)
        kv = lax.dot_general(ke[r0:r1], v[r0:r1], (((0,), (0,)), ((), ())),
                             preferred_element_type=F32)
        s_ref[...] = s * jnp.exp(tot_col) + kv
    return jnp.concatenate(outs, axis=0)


def _gla_kernel(*refs, reverse):
    if reverse:
        (q_ref, k_ref, v_ref, lr_ref, w_ref, b_ref, of_ref, r_ref, ng_ref, y_ref, s_ref) = refs
    else:
        (q_ref, k_ref, v_ref, lr_ref, w_ref, b_ref, o_ref, s_ref) = refs
    step = pl.program_id(2)

    @pl.when(step == 0)
    def _():
        s_ref[...] = jnp.zeros_like(s_ref)

    c = GLA_CHUNK
    n_head, dk, dv = s_ref.shape
    n_row = q_ref.shape[0]
    z = jnp.dot(lr_ref[...], w_ref[...], preferred_element_type=F32) + b_ref[...]
    g = _log_sigmoid(z) * (1.0 / GLA_TAU)
    row = lax.broadcasted_iota(jnp.int32, (n_row, n_row), 0)
    col = lax.broadcasted_iota(jnp.int32, (n_row, n_row), 1)
    shift = c.bit_length() - 1
    same_b = (row >> shift) == (col >> shift)
    same = same_b.astype(F32)
    tri = jnp.logical_and(same_b, (col >= row) if reverse else (col <= row)).astype(F32)
    eye = (lax.broadcasted_iota(jnp.int32, (dk, dk), 0)
           == lax.broadcasted_iota(jnp.int32, (dk, dk), 1))
    for hh in range(n_head):
        kc = pl.ds(hh * dk, dk)
        vc = pl.ds(hh * dv, dv)
        o = _gla_block(q_ref[:, kc].astype(F32), k_ref[:, kc].astype(F32), v_ref[:, vc],
                       g[:, hh * dk:(hh + 1) * dk], s_ref.at[hh], tri, same, eye, reverse)
        if reverse:
            o = o + of_ref[:, vc]
            o = o * lax.rsqrt(jnp.mean(o * o, axis=-1, keepdims=True) + EPS)
            y_ref[:, vc] = (o * ng_ref[...] * r_ref[:, vc].astype(F32)).astype(y_ref.dtype)
        else:
            o_ref[:, vc] = o


def gla_scan(qkv, lr, r_act, o_fwd, w_pad, bias, norm_g, n_batch, n_ctx_blocks, reverse):
    t = qkv.shape[0]
    dkh = w_pad.shape[1] // GLA_HEADS
    dv = qkv.shape[1] - 2 * w_pad.shape[1]
    dvh = dv // GLA_HEADS
    blk = GLA_BLOCK
    ctx_per_batch = n_ctx_blocks // n_batch
    assert ctx_per_batch == 1
    lat_per_batch = (t // blk - n_ctx_blocks) // n_batch
    n_step = ctx_per_batch + lat_per_batch

    def row_blk(b, s):
        if reverse:
            lat = n_ctx_blocks + lat_per_batch * b + (lat_per_batch - s)
        else:
            lat = n_ctx_blocks + lat_per_batch * b + (s - 1)
        return jnp.where(s == 0, b, lat)

    hp = GLA_HEADS_PER_STEP
    kw, vw = hp * dkh, hp * dvh
    k_off = w_pad.shape[1] // kw
    v_off = 2 * w_pad.shape[1] // vw
    in_specs = [pl.BlockSpec((blk, kw), lambda b, h, s: (row_blk(b, s), h)),
                pl.BlockSpec((blk, kw), lambda b, h, s: (row_blk(b, s), k_off + h)),
                pl.BlockSpec((blk, vw), lambda b, h, s: (row_blk(b, s), v_off + h)),
                pl.BlockSpec((blk, LANES), lambda b, h, s: (row_blk(b, s), 0)),
                pl.BlockSpec((LANES, kw), lambda b, h, s: (0, h)),
                pl.BlockSpec((1, kw), lambda b, h, s: (0, h))]
    args = [qkv, qkv, qkv, lr, w_pad, bias]
    if reverse:
        in_specs += [pl.BlockSpec((blk, vw), lambda b, h, s: (row_blk(b, s), h)),
                     pl.BlockSpec((blk, vw), lambda b, h, s: (row_blk(b, s), h)),
                     pl.BlockSpec((1, dvh), lambda b, h, s: (0, 0))]
        args += [o_fwd, r_act, norm_g]
        out_dtype = BF16
    else:
        out_dtype = F32
    return pl.pallas_call(
        functools.partial(_gla_kernel, reverse=reverse),
        grid=(n_batch, GLA_HEADS // hp, n_step),
        in_specs=in_specs,
        out_specs=pl.BlockSpec((blk, vw), lambda b, h, s: (row_blk(b, s), h)),
        out_shape=jax.ShapeDtypeStruct((t, dv), out_dtype),
        scratch_shapes=[pltpu.VMEM((hp, dkh, dvh), F32)],
        compiler_params=_cparams(32, 3),
        name="gla_bwd" if reverse else "gla_fwd",
    )(*args)


def _gmlp_kernel(u_ref, v_ref, ws_ref, bt_ref, o_ref):
    v = v_ref[...].astype(F32)
    mu = jnp.mean(v, axis=-1, keepdims=True)
    vc = v - mu
    var = jnp.mean(vc * vc, axis=-1, keepdims=True)
    vn = (vc * lax.rsqrt(var + EPS)).astype(BF16)
    n_chunk = v.shape[0] // A_CHUNK
    for gi in range(A_GROUPS):
        wsg = ws_ref[gi].astype(BF16)
        bias = bt_ref[:, gi:gi + 1]
        cols = slice(gi * A_GROUP_W, (gi + 1) * A_GROUP_W)
        for n in range(n_chunk):
            rows = slice(n * A_CHUNK, (n + 1) * A_CHUNK)
            s = jnp.dot(wsg, vn[rows, cols], preferred_element_type=F32) + bias
            o_ref[rows, cols] = (u_ref[rows, cols].astype(F32) * s).astype(o_ref.dtype)


def gmlp(uv, a_ws, a_bias_t):
    t = uv.shape[0]
    aw = uv.shape[1] // 2
    tm = 512
    return pl.pallas_call(
        _gmlp_kernel,
        grid=(t // tm,),
        in_specs=[pl.BlockSpec((tm, aw), lambda i: (i, 0)),
                  pl.BlockSpec((tm, aw), lambda i: (i, 1)),
                  pl.BlockSpec(a_ws.shape, lambda i: (0, 0, 0)),
                  pl.BlockSpec(a_bias_t.shape, lambda i: (0, 0))],
        out_specs=pl.BlockSpec((tm, aw), lambda i: (i, 0)),
        out_shape=jax.ShapeDtypeStruct((t, aw), BF16),
        compiler_params=_cparams(32, 1),
        name="gmlp",
    )(uv, uv, a_ws, a_bias_t)


def _merge_kernel(a_ref, y_ref, ga_ref, gb_ref, pa_ref, pb_ref, o_ref, pab_ref, pbb_ref):
    @pl.when(pl.program_id(1) == 0)
    def _():
        pab_ref[...] = pa_ref[...].astype(BF16)
        pbb_ref[...] = pb_ref[...].astype(BF16)

    ya = jnp.dot(a_ref[...], pab_ref[...], preferred_element_type=F32)
    yb = jnp.dot(y_ref[...], pbb_ref[...], preferred_element_type=F32)
    o_ref[...] = (ga_ref[...].astype(F32) * ya + gb_ref[...].astype(F32) * yb).astype(o_ref.dtype)


def merge_proj(a, yb, gates, p_a, p_b, layer):
    t, d = yb.shape
    tm, tn = ROW_TILE, COL_TILE
    nj = d // tn
    ka, kb = p_a.shape[1], p_b.shape[1]
    return pl.pallas_call(
        _merge_kernel,
        grid=(nj, t // tm),
        in_specs=[pl.BlockSpec((tm, ka), lambda j, i: (i, 0)),
                  pl.BlockSpec((tm, kb), lambda j, i: (i, 0)),
                  pl.BlockSpec((tm, tn), lambda j, i: (i, j)),
                  pl.BlockSpec((tm, tn), lambda j, i: (i, nj + j)),
                  pl.BlockSpec((None, ka, tn), lambda j, i: (layer, 0, j)),
                  pl.BlockSpec((None, kb, tn), lambda j, i: (layer, 0, j))],
        out_specs=pl.BlockSpec((tm, tn), lambda j, i: (i, j)),
        out_shape=jax.ShapeDtypeStruct((t, d), BF16),
        scratch_shapes=[pltpu.VMEM((ka, tn), BF16), pltpu.VMEM((kb, tn), BF16)],
        compiler_params=_cparams(48, 2),
        name="merge_proj",
    )(a, yb, gates, gates, p_a, p_b)


def _out_proj_kernel(*refs, n_ctx_tiles):
    m_ref, w_ref = refs[:2]
    gt_ref, o_ref, wb_ref = refs[-3:]
    i = pl.program_id(1)

    @pl.when(i == 0)
    def _():
        wb_ref[...] = w_ref[...].astype(BF16)

    y = jnp.dot(m_ref[...], wb_ref[...], preferred_element_type=F32)
    o_ref[...] = _stream_tile(refs[2:-3], i, n_ctx_tiles) + gt_ref[0] * y


def out_proj(mrg, w_out, layer, stream, mods, gt_row, mod_idx):
    t, d = mrg.shape
    tm, tn = ROW_TILE, COL_TILE
    specs, arrays, nct = _stream_specs(stream, tm, tn, lambda j, i: i, lambda j, i: j)
    return pl.pallas_call(
        functools.partial(_out_proj_kernel, n_ctx_tiles=nct),
        grid=(d // tn, t // tm),
        in_specs=[pl.BlockSpec((tm, d), lambda j, i: (i, 0)),
                  pl.BlockSpec((None, d, tn), lambda j, i: (layer, 0, j))] + specs + [
                  pl.BlockSpec((1, 1, tn), lambda j, i: (gt_row + mod_idx(i), 0, j))],
        out_specs=pl.BlockSpec((tm, tn), lambda j, i: (i, j)),
        out_shape=jax.ShapeDtypeStruct((t, d), F32),
        scratch_shapes=[pltpu.VMEM((d, tn), BF16)],
        compiler_params=_cparams(48, 2),
        name="out_proj",
    )(mrg, w_out, *arrays, mods)


def _router_kernel(x_ref, g_ref, sh_ref, sc_ref, wr_ref, br_ref,
                   h_ref, idx_ref, gate_ref, rank_ref, cnt_ref, carry_ref):
    @pl.when(pl.program_id(0) == 0)
    def _():
        carry_ref[...] = jnp.zeros_like(carry_ref)

    x = x_ref[...]
    r = lax.rsqrt(jnp.mean(x * x, axis=-1, keepdims=True) + EPS)
    h = (x * r) * g_ref[...]
    h = h * (1.0 + sc_ref[0]) + sh_ref[0]
    n_piece = h.shape[1] // LANES
    for p in range(n_piece):
        h_ref[pl.ds(p, h.shape[0], stride=n_piece), :] = h[:, p * LANES:(p + 1) * LANES]
    logits = jnp.dot(h, wr_ref[...], preferred_element_type=F32) + br_ref[...]
    tm = x.shape[0]
    lane = lax.broadcasted_iota(jnp.int32, (tm, LANES), 1)
    lane_f = lane.astype(F32)
    member = jnp.zeros((tm, LANES), F32)
    vals, sels = [], []
    idx_out = jnp.zeros((tm, LANES), F32)
    for kk in range(TOP_K):
        m = jnp.max(logits, axis=1, keepdims=True)
        ik = jnp.min(jnp.where(logits == m, lane_f, float(LANES)), axis=1, keepdims=True)
        sel = lane_f == ik
        logits = jnp.where(sel, -jnp.inf, logits)
        member = member + sel.astype(F32)
        idx_out = jnp.where(lane == kk, ik, idx_out)
        vals.append(m)
        sels.append(sel)
    es = [jnp.exp(v - vals[0]) for v in vals]
    denom = es[0] + es[1] + es[2] + es[3]
    gate_out = jnp.zeros((tm, LANES), F32)
    for kk in range(TOP_K):
        gate_out = jnp.where(lane == kk, es[kk] / denom, gate_out)
    rr = lax.broadcasted_iota(jnp.int32, (tm, tm), 0)
    cc = lax.broadcasted_iota(jnp.int32, (tm, tm), 1)
    strict = (cc < rr).astype(BF16)
    before = jnp.dot(strict, member.astype(BF16), preferred_element_type=F32) + carry_ref[...]
    rank_out = jnp.zeros((tm, LANES), F32)
    for kk in range(TOP_K):
        rk = jnp.sum(jnp.where(sels[kk], before, 0.0), axis=1, keepdims=True)
        rank_out = jnp.where(lane == kk, rk, rank_out)
    carry_ref[...] = carry_ref[...] + jnp.sum(member, axis=0, keepdims=True)
    idx_ref[...] = idx_out.astype(jnp.int32)
    gate_ref[...] = gate_out
    rank_ref[...] = rank_out.astype(jnp.int32)
    cnt_ref[...] = carry_ref[...]


def router(xa, row0_tiles, gain, mods, sh_row, sc_row, mod_idx, wr_pad, br_pad):
    t, d = xa.shape
    tm = 512
    per = ROW_TILE // tm
    off = row0_tiles * per
    n = t // tm - off
    tl = n * tm
    small = lambda dt: jax.ShapeDtypeStruct((tl, LANES), dt)
    return pl.pallas_call(
        _router_kernel,
        grid=(n,),
        in_specs=[pl.BlockSpec((tm, d), lambda i: (i + off, 0)),
                  pl.BlockSpec((1, d), lambda i: (0, 0)),
                  pl.BlockSpec((1, 1, d), lambda i: (sh_row + mod_idx((i + off) // per), 0, 0)),
                  pl.BlockSpec((1, 1, d), lambda i: (sc_row + mod_idx((i + off) // per), 0, 0)),
                  pl.BlockSpec((d, LANES), lambda i: (0, 0)),
                  pl.BlockSpec((1, LANES), lambda i: (0, 0))],
        out_specs=[pl.BlockSpec((tm * (d // LANES), LANES), lambda i: (i, 0)),
                   pl.BlockSpec((tm, LANES), lambda i: (i, 0)),
                   pl.BlockSpec((tm, LANES), lambda i: (i, 0)),
                   pl.BlockSpec((tm, LANES), lambda i: (i, 0)),
                   pl.BlockSpec((1, LANES), lambda i: (0, 0))],
        out_shape=[jax.ShapeDtypeStruct((tl * (d // LANES), LANES), F32), small(jnp.int32), small(F32),
                   small(jnp.int32), jax.ShapeDtypeStruct((1, LANES), F32)],
        scratch_shapes=[pltpu.VMEM((1, LANES), F32)],
        compiler_params=_cparams(40, 1),
        name="router",
    )(xa, gain.reshape(1, d), mods, mods, wr_pad, br_pad)


DMA_UNROLL = 8


def _row_gather(idx_ref, n_rows, src_ref, dst_ref, sem, run=1, priority=0):
    pitch = dst_ref.shape[0] // n_rows

    def body(g, carry):
        for u in range(DMA_UNROLL):
            r = g * DMA_UNROLL + u
            first = idx_ref[0, 0, r] if run == 1 else pl.multiple_of(idx_ref[0, 0, r], run)
            pltpu.async_copy(src_ref.at[pl.ds(first, run)], dst_ref.at[pl.ds(r * pitch, run)], sem,
                             priority=priority)
        return carry
    lax.fori_loop(0, n_rows // DMA_UNROLL, body, 0)


def _expert_kernel(be_ref, bv_ref, na_ref, cur_ref, nxt_ref, h_ref, w1g_ref, w1l_ref, b1_ref,
                   w2_ref, b2_ref, o_ref, gbuf, xs, act_s, sem, *, nf1):
    b = pl.program_id(0)
    j = pl.program_id(1)
    n_act = na_ref[0]
    active = b < n_act
    rows = EXPERT_ROWS
    n_sub = (bv_ref[b] + EXPERT_SUB - 1) // EXPERT_SUB
    jj = jnp.minimum(j, nf1 - 1)

    n_piece = xs.shape[1] // LANES
    pitch = gbuf.shape[0] // rows
    e = be_ref[jnp.minimum(b, n_act - 1)]
    tf, tn = act_s.shape[2], o_ref.shape[1]
    b1g = b1_ref[e, :, pl.ds(pl.multiple_of(jj * tf, tf), tf)]
    b1l = b1_ref[e, :, pl.ds(pl.multiple_of((nf1 + jj) * tf, tf), tf)]
    b2 = b2_ref[e, :, pl.ds(pl.multiple_of(jnp.maximum(j - nf1, 0) * tn, tn), tn)]

    @pl.when(jnp.logical_and(active, j == 0))
    def _():
        @pl.when(b == 0)
        def _():
            _row_gather(cur_ref, rows, h_ref, gbuf, sem, run=n_piece)

        pltpu.make_async_copy(h_ref.at[pl.ds(0, rows * n_piece)],
                              gbuf.at[pl.ds(0, rows * n_piece)], sem).wait()
        for p in range(n_piece):
            xs[:, p * LANES:(p + 1) * LANES] = gbuf[pl.ds(p, rows, stride=pitch), :].astype(BF16)

        @pl.when(b + 1 < n_act)
        def _():
            _row_gather(nxt_ref, rows, h_ref, gbuf, sem, run=n_piece, priority=1)

    @pl.when(jnp.logical_and(jnp.logical_not(active), j >= nf1))
    def _():
        o_ref[...] = jnp.zeros_like(o_ref)

    for m in range(1, rows // EXPERT_SUB + 1):
        r = m * EXPERT_SUB

        @pl.when(jnp.logical_and(active, jnp.logical_and(j < nf1, n_sub == m)))
        def _():
            x = xs[0:r, :]
            hg = jnp.dot(x, w1g_ref[0].astype(BF16), preferred_element_type=F32) + b1g
            hl = jnp.dot(x, w1l_ref[0].astype(BF16), preferred_element_type=F32) + b1l
            hg = jnp.minimum(hg, SWIGLU_LIMIT)
            hl = jnp.clip(hl, -SWIGLU_LIMIT, SWIGLU_LIMIT)
            act = hg * _sigmoid(SWIGLU_ALPHA * hg) * (hl + 1.0)
            act_s[jj, 0:r, :] = act.astype(BF16)

        @pl.when(jnp.logical_and(active, jnp.logical_and(j >= nf1, n_sub == m)))
        def _():
            a = jnp.concatenate([act_s[t, 0:r, :] for t in range(nf1)], axis=1)
            o_ref[0:r, :] = jnp.dot(a, w2_ref[0].astype(BF16),
                                    preferred_element_type=F32) + b2
            if r < rows:
                o_ref[r:rows, :] = jnp.zeros((rows - r, o_ref.shape[1]), o_ref.dtype)


def experts(h, src_tok, blk_e, blk_valid, n_active, w1, b1, w2, b2, layer):
    _, n_exp, d, f2 = w1.shape
    f = f2 // 2
    tf, tn = EXPERT_FT, EXPERT_NT
    nf1, nf2 = f // tf, d // tn
    n_blk = blk_e.shape[0]
    rows = EXPERT_ROWS

    def eb(b, be, na):
        return be[jnp.minimum(b, na[0] - 1)]

    def j1(b, j, na):
        return jnp.where(b < na[0], jnp.minimum(j, nf1 - 1), nf1 - 1)

    def w2_idx(b, j, be, na):
        cur = jnp.logical_and(b < na[0], j >= nf1)
        e_prev = be[jnp.maximum(jnp.minimum(b, na[0]) - 1, 0)]
        return (layer, jnp.where(cur, eb(b, be, na), e_prev), 0, jnp.where(cur, j - nf1, nf2 - 1))

    smem_rows = functools.partial(pl.BlockSpec, (1, 1, rows), memory_space=pltpu.SMEM)
    in_specs = [
        smem_rows(lambda b, j, be, bv, na: (b, 0, 0)),
        smem_rows(lambda b, j, be, bv, na: (jnp.minimum(b + 1, n_blk - 1), 0, 0)),
        pl.BlockSpec(memory_space=pl.ANY),
        pl.BlockSpec((None, 1, d, tf), lambda b, j, be, bv, na: (layer, eb(b, be, na), 0, j1(b, j, na))),
        pl.BlockSpec((None, 1, d, tf), lambda b, j, be, bv, na: (layer, eb(b, be, na), 0, nf1 + j1(b, j, na))),
        pl.BlockSpec((n_exp, 1, f2), lambda b, j, be, bv, na: (0, 0, 0)),
        pl.BlockSpec((None, 1, f, tn), lambda b, j, be, bv, na: w2_idx(b, j, be, na)),
        pl.BlockSpec((n_exp, 1, d), lambda b, j, be, bv, na: (0, 0, 0)),
    ]
    return pl.pallas_call(
        functools.partial(_expert_kernel, nf1=nf1),
        grid_spec=pltpu.PrefetchScalarGridSpec(
            num_scalar_prefetch=3,
            grid=(n_blk, nf1 + nf2),
            in_specs=in_specs,
            out_specs=pl.BlockSpec((rows, tn), lambda b, j, be, bv, na: (b, jnp.maximum(j - nf1, 0))),
            scratch_shapes=[pltpu.VMEM((rows * (d // LANES + 1), LANES), F32), pltpu.VMEM((rows, d), BF16),
                            pltpu.VMEM((nf1, rows, tf), BF16), pltpu.SemaphoreType.DMA(())]),
        out_shape=jax.ShapeDtypeStruct((n_blk * rows, d), F32),
        compiler_params=_cparams(58, 2),
        name="moe_experts",
    )(blk_e, blk_valid, n_active, src_tok, src_tok, h, w1, w1, b1.reshape(n_exp, 1, f2),
      w2, b2.reshape(n_exp, 1, d))


def _combine_kernel(dest_ref, nxt_ref, y_ref, x_ref, gate_ref, gt_ref, ng_ref, o_ref, buf_ref, sem,
                    *, final_norm):
    i = pl.program_id(0)
    tm = x_ref.shape[0]
    n_rows = tm * TOP_K
    slot = i % 2

    @pl.when(i == 0)
    def _():
        _row_gather(dest_ref, n_rows, y_ref, buf_ref.at[0], sem.at[0])

    @pl.when(i + 1 < pl.num_programs(0))
    def _():
        _row_gather(nxt_ref, n_rows, y_ref, buf_ref.at[1 - slot], sem.at[1 - slot])

    pltpu.make_async_copy(y_ref.at[pl.ds(0, n_rows)], buf_ref.at[slot], sem.at[slot]).wait()
    gate = gate_ref[...]
    acc = jnp.zeros(x_ref.shape, F32)
    for kk in range(TOP_K):
        acc = acc + gate[:, kk:kk + 1] * buf_ref[slot, pl.ds(kk * tm, tm), :]
    xn = x_ref[...] + gt_ref[0] * acc
    if final_norm:
        xn = xn * lax.rsqrt(jnp.mean(xn * xn, axis=-1, keepdims=True) + EPS) * ng_ref[...]
    o_ref[...] = xn


def combine(y_sorted, dest_km, xa, row0_tiles, gate, mods, gt_row, mod_idx, norm_g, final_norm):
    t, d = xa.shape
    tm = 256
    per = ROW_TILE // tm
    off = row0_tiles * per
    n = t // tm - off
    out_rows = n * tm if final_norm else t
    out_off = 0 if final_norm else off
    kernel = functools.partial(_combine_kernel, final_norm=final_norm)
    call = pl.pallas_call(
        kernel,
        grid=(n,),
        in_specs=[pl.BlockSpec((1, 1, TOP_K * tm), lambda i: (i, 0, 0), memory_space=pltpu.SMEM),
                  pl.BlockSpec((1, 1, TOP_K * tm), lambda i: (jnp.minimum(i + 1, n - 1), 0, 0),
                               memory_space=pltpu.SMEM),
                  pl.BlockSpec(memory_space=pl.ANY),
                  pl.BlockSpec((tm, d), lambda i: (i + off, 0)),
                  pl.BlockSpec((tm, LANES), lambda i: (i, 0)),
                  pl.BlockSpec((1, 1, d), lambda i: (gt_row + mod_idx((i + off) // per), 0, 0)),
                  pl.BlockSpec((1, d), lambda i: (0, 0))],
        out_specs=pl.BlockSpec((tm, d), lambda i: (i + out_off, 0)),
        out_shape=jax.ShapeDtypeStruct((out_rows, d), F32),
        scratch_shapes=[pltpu.VMEM((2, TOP_K * tm, d), F32), pltpu.SemaphoreType.DMA((2,))],
        compiler_params=_cparams(40, 1),
        name="moe_combine",
    )
    return call(dest_km, dest_km, y_sorted, xa, gate, mods, norm_g.reshape(1, d))


def _routing_tables(eidx, rank, counts, n_blk):
    t = eidx.shape[0]
    rows = EXPERT_ROWS
    padded = (counts + rows - 1) // rows * rows
    p_ends = jnp.cumsum(padded)
    p_starts = p_ends - padded
    starts = jnp.cumsum(counts) - counts
    dest = p_starts[eidx] + rank
    n_active = (p_ends[-1] // rows).astype(jnp.int32).reshape(1)
    blk_start = jnp.arange(n_blk, dtype=jnp.int32) * rows
    blk_e = jnp.minimum(jnp.searchsorted(p_ends, blk_start, side="right"), N_EXPERTS - 1).astype(jnp.int32)
    blk_valid = jnp.clip(counts[blk_e] - (blk_start - p_starts[blk_e]), 0, rows).astype(jnp.int32)
    order = jnp.argsort(eidx.reshape(-1), stable=True).astype(jnp.int32)
    tok_sorted = order // TOP_K
    within = (blk_start - p_starts[blk_e])[:, None] + jnp.arange(rows, dtype=jnp.int32)[None, :]
    pos = jnp.clip(starts[blk_e][:, None] + within, 0, t * TOP_K - 1)
    src = jnp.where(within < counts[blk_e][:, None], tok_sorted[pos], 0)
    return dest, src.reshape(n_blk, 1, rows).astype(jnp.int32), blk_e, blk_valid, n_active


def moe_ffn(xa, row0_tiles, gain, mods, base_row, mod_idx, wr_pad, br_pad, w1, b1, w2, b2, layer,
            norm_final, final_norm):
    n_b = 8
    h, eidx, gate, rank, cnt = router(xa, row0_tiles, gain, mods, base_row + 3 * n_b,
                                      base_row + 4 * n_b, mod_idx, wr_pad, br_pad)
    tl = eidx.shape[0]
    eidx4 = eidx[:, :TOP_K]
    rank4 = rank[:, :TOP_K]
    counts = cnt[0, :N_EXPERTS].astype(jnp.int32)
    n_blk = (tl * TOP_K + N_EXPERTS * (EXPERT_ROWS - 1)) // EXPERT_ROWS
    dest, src, blk_e, blk_valid, n_active = _routing_tables(eidx4, rank4, counts, n_blk)
    ys = experts(h, src * (w1.shape[2] // LANES), blk_e, blk_valid, n_active, w1, b1, w2, b2, layer)
    tmc = 256
    dest_km = dest.reshape(tl // tmc, tmc, TOP_K).transpose(0, 2, 1).reshape(tl // tmc, 1, TOP_K * tmc)
    return combine(ys, dest_km.astype(jnp.int32), xa, row0_tiles, gate, mods, base_row + 5 * n_b,
                   mod_idx, norm_final, final_norm)


def kernel(x, c, ctx, c_ctx, norm_mix, norm_ffn, w_ada, b_ada, w_in, conv_w, a_ws, a_bias, gla_wf, gla_bf, gla_wb, gla_bb, gla_norm, p_a, p_b, w_out, w_router, b_router, w1, b1, w2, b2, norm_final):
    n_b, seq, d = x.shape
    n_ctx = ctx.shape[1]
    depth = w_ada.shape[0]
    assert n_b * n_ctx == ROW_TILE and seq % ROW_TILE == 0 and n_b < 8
    tiles_per_batch = seq // ROW_TILE
    a_width = a_ws.shape[1] * a_ws.shape[2]
    dk = gla_wf.shape[2]
    dv = d

    def mod_idx(tile):
        return jnp.where(tile == 0, n_b, (tile - 1) // tiles_per_batch)

    xa = (ctx.reshape(n_b * n_ctx, d), x.reshape(n_b * seq, d))
    cond = jnp.zeros((8, d), F32).at[:n_b].set(c).at[n_b].set(c_ctx)
    mods_all = ada_mods(cond, w_ada, b_ada)
    mods = mods_all.reshape(depth, 8, N_MOD, d).transpose(0, 2, 1, 3).reshape(depth * N_MOD * 8, 1, d)

    c_u, c_q, c_r, c_lr, c_g = 0, 2 * a_width, 2 * a_width + 2 * dk + dv, 2 * a_width + 2 * dk + 2 * dv, \
        2 * a_width + 2 * dk + 2 * dv + 2 * GLA_RANK
    w_in_t = jnp.swapaxes(w_in, 1, 2)
    out = None
    for l in range(depth):
        last = l == depth - 1
        base = l * N_MOD * 8
        w_lr = jnp.zeros((1, LANES, d), F32).at[0, :2 * GLA_RANK].set(
            w_in_t[l, c_lr:c_lr + 2 * GLA_RANK])
        w_g = w_in_t[l:l + 1, c_g:]
        wf_pad = jnp.zeros((LANES, dk), F32).at[:GLA_RANK].set(gla_wf[l])
        wb_pad = jnp.zeros((LANES, dk), F32).at[GLA_RANK:2 * GLA_RANK].set(gla_wb[l])

        h = norm_mod(xa, norm_mix[l], mods, base + 0 * 8, base + 1 * 8, mod_idx)
        uv = matmul_act(h, w_in_t, l, c_u, 2 * a_width, act="gelu")
        qkv = matmul_act(h, w_in_t, l, c_q, 2 * dk + dv, act="conv", conv_w=conv_w[l],
                         ctx_seg=n_ctx, q_cols=dk, q_scale=float(dk // GLA_HEADS) ** -0.5)
        r_act = matmul_act(h, w_in_t, l, c_r, dv, act="silu")
        lr = matmul_act(h, w_lr, 0, 0, LANES, act="none", out_dtype=F32, tn=LANES)
        gates = matmul_act(h, w_g, 0, 0, 2 * d, act="sigmoid")

        n_ctx_blocks = n_b * n_ctx // GLA_BLOCK
        o_f = gla_scan(qkv, lr, None, None, wf_pad, gla_bf[l].reshape(1, dk), None,
                       n_b, n_ctx_blocks, reverse=False)
        y_gla = gla_scan(qkv, lr, r_act, o_f, wb_pad, gla_bb[l].reshape(1, dk),
                         gla_norm[l].reshape(1, -1), n_b, n_ctx_blocks, reverse=True)
        a = gmlp(uv, a_ws[l], a_bias[l].T)
        mrg = merge_proj(a, y_gla, gates, p_a, p_b, l)
        xa = out_proj(mrg, w_out, l, xa, mods, base + 2 * 8, mod_idx)

        wr_pad = jnp.zeros((d, LANES), F32).at[:, :N_EXPERTS].set(w_router[l])
        br_pad = jnp.full((1, LANES), -jnp.inf, F32).at[0, :N_EXPERTS].set(b_router[l])
        res = moe_ffn(xa, 1 if last else 0, norm_ffn[l], mods, base, mod_idx, wr_pad, br_pad,
                      w1, b1[l], w2, b2[l], l, norm_final, last)
        if last:
            out = res
        else:
            xa = res
    return out.reshape(n_b, seq, d)
```

```python
import functools

import jax
import jax.numpy as jnp
from jax import lax
from jax.experimental import pallas as pl
from jax.experimental.pallas import tpu as pltpu

F32 = jnp.float32
BF16 = jnp.bfloat16

GRID_W = 64
EPS = 1e-6
N_MOD = 6
A_GROUPS = 8
A_GROUP_W = 128
A_CHUNK = 128
GLA_HEADS = 4
GLA_RANK = 16
GLA_TAU = 16.0
GLA_CHUNK = 64
N_EXPERTS = 32
TOP_K = 4
SWIGLU_LIMIT = 7.0
SWIGLU_ALPHA = 1.702

LANES = 128
ROW_TILE = 1024
COL_TILE = 512
GLA_BLOCK = 256
GLA_HEADS_PER_STEP = 4
EXPERT_ROWS = 1024
EXPERT_SUB = 256
EXPERT_FT = 512
EXPERT_NT = 512
MIB = 1024 * 1024


def _cparams(vmem_mib, n_axes):
    return pltpu.CompilerParams(
        dimension_semantics=("arbitrary",) * n_axes,
        vmem_limit_bytes=int(vmem_mib * MIB))


def _sigmoid(t):
    return 0.5 * (1.0 + jnp.tanh(0.5 * t))


def _silu(t):
    return t * _sigmoid(t)


def _gelu_tanh(t):
    return 0.5 * t * (1.0 + jnp.tanh(0.7978845608028654 * (t + 0.044715 * (t * t * t))))


def _log_sigmoid(z):
    return jnp.minimum(z, 0.0) - jnp.log(1.0 + jnp.exp(-jnp.abs(z)))


def _ada_kernel(c_ref, w_ref, b_ref, o_ref):
    s = _silu(c_ref[...]).astype(BF16)
    w = w_ref[0].astype(BF16)
    o_ref[0] = jnp.dot(s, w, preferred_element_type=F32) + b_ref[0]


def ada_mods(cond, w_ada, b_ada):
    n_layer, d, n = w_ada.shape
    tn = 1024
    return pl.pallas_call(
        _ada_kernel,
        grid=(n_layer, n // tn),
        in_specs=[pl.BlockSpec((8, d), lambda l, j: (0, 0)),
                  pl.BlockSpec((1, d, tn), lambda l, j: (l, 0, j)),
                  pl.BlockSpec((1, 1, tn), lambda l, j: (l, 0, j))],
        out_specs=pl.BlockSpec((1, 8, tn), lambda l, j: (l, 0, j)),
        out_shape=jax.ShapeDtypeStruct((n_layer, 8, n), F32),
        compiler_params=_cparams(32, 2),
        name="ada_mods",
    )(cond, w_ada, b_ada.reshape(n_layer, 1, n))


def _stream_specs(stream, tm, tn, row_of, col_of):
    if not isinstance(stream, tuple):
        return [pl.BlockSpec((tm, tn), lambda *g: (row_of(*g), col_of(*g)))], [stream], 0
    ctx2d, lat2d = stream
    nct = ctx2d.shape[0] // tm
    specs = [pl.BlockSpec((tm, tn), lambda *g: (jnp.minimum(row_of(*g), nct - 1), col_of(*g))),
             pl.BlockSpec((tm, tn), lambda *g: (jnp.maximum(row_of(*g) - nct, 0), col_of(*g)))]
    return specs, [ctx2d, lat2d], nct


def _stream_tile(refs, tile, n_ctx_tiles):
    if len(refs) == 1:
        return refs[0][...]
    return jnp.where(tile < n_ctx_tiles, refs[0][...], refs[1][...])


def _norm_mod_kernel(*refs, n_ctx_tiles):
    g_ref, sh_ref, sc_ref, o_ref = refs[-4:]
    x = _stream_tile(refs[:-4], pl.program_id(0), n_ctx_tiles)
    r = lax.rsqrt(jnp.mean(x * x, axis=-1, keepdims=True) + EPS)
    h = (x * r) * g_ref[...]
    h = h * (1.0 + sc_ref[0]) + sh_ref[0]
    o_ref[...] = h.astype(o_ref.dtype)


def norm_mod(stream, gain, mods, sh_row, sc_row, mod_idx):
    t = sum(s.shape[0] for s in stream) if isinstance(stream, tuple) else stream.shape[0]
    d = gain.shape[0]
    tm = 512
    per = ROW_TILE // tm
    specs, arrays, nct = _stream_specs(stream, tm, d, lambda i: i, lambda i: 0)
    return pl.pallas_call(
        functools.partial(_norm_mod_kernel, n_ctx_tiles=nct),
        grid=(t // tm,),
        in_specs=specs + [
            pl.BlockSpec((1, d), lambda i: (0, 0)),
            pl.BlockSpec((1, 1, d), lambda i: (sh_row + mod_idx(i // per), 0, 0)),
            pl.BlockSpec((1, 1, d), lambda i: (sc_row + mod_idx(i // per), 0, 0))],
        out_specs=pl.BlockSpec((tm, d), lambda i: (i, 0)),
        out_shape=jax.ShapeDtypeStruct((t, d), BF16),
        compiler_params=_cparams(40, 1),
        name="norm_mod",
    )(*arrays, gain.reshape(1, d), mods, mods)


def _mm_kernel(*refs, act, ctx_seg, q_tiles, q_scale):
    if act == "conv":
        x_ref, w_ref, cw_ref, o_ref, wb_ref = refs
    else:
        x_ref, w_ref, o_ref, wb_ref = refs
    j = pl.program_id(0)
    i = pl.program_id(1)

    @pl.when(i == 0)
    def _():
        wb_ref[...] = w_ref[...].T.astype(BF16)

    y = jnp.dot(x_ref[...], wb_ref[...], preferred_element_type=F32)
    if act == "gelu":
        y = _gelu_tanh(y)
    elif act == "silu":
        y = _silu(y.astype(o_ref.dtype))
    elif act == "sigmoid":
        y = _sigmoid(y.astype(o_ref.dtype))
    elif act == "conv":
        seg = jnp.where(i == 0, ctx_seg, GRID_W)
        pos = lax.broadcasted_iota(jnp.int32, y.shape, 0) & (seg - 1)
        prev = jnp.where(pos == 0, 0.0, pltpu.roll(y, 1, 0))
        nxt = jnp.where(pos == seg - 1, 0.0, pltpu.roll(y, y.shape[0] - 1, 0))
        cw = cw_ref[...]
        y = _silu((cw[0:1] * prev + cw[1:2] * y + cw[2:3] * nxt).astype(o_ref.dtype))
        y = y * jnp.where(j < q_tiles, q_scale, 1.0).astype(o_ref.dtype)
    o_ref[...] = y.astype(o_ref.dtype)


def matmul_act(h, w_t, layer, col0, ncols, act="none", out_dtype=BF16, conv_w=None,
               ctx_seg=256, q_cols=0, q_scale=1.0, tn=COL_TILE):
    t, k = h.shape
    tm = ROW_TILE
    off = col0 // tn
    in_specs = [pl.BlockSpec((tm, k), lambda j, i: (i, 0)),
                pl.BlockSpec((None, tn, k), lambda j, i: (layer, j + off, 0))]
    args = [h, w_t]
    if act == "conv":
        in_specs.append(pl.BlockSpec((3, tn), lambda j, i: (0, j)))
        args.append(conv_w)
    return pl.pallas_call(
        functools.partial(_mm_kernel, act=act, ctx_seg=ctx_seg, q_tiles=q_cols // tn,
                          q_scale=q_scale),
        grid=(ncols // tn, t // tm),
        in_specs=in_specs,
        out_specs=pl.BlockSpec((tm, tn), lambda j, i: (i, j)),
        out_shape=jax.ShapeDtypeStruct((t, ncols), out_dtype),
        scratch_shapes=[pltpu.VMEM((k, tn), BF16)],
        compiler_params=_cparams(48, 2),
        name="matmul_" + act,
    )(*args)


def _gla_block(q, k, v, g, s_ref, tri, same, eye, reverse):
    c = GLA_CHUNK
    n_chunk = q.shape[0] // c
    cum = jnp.dot(tri, g, preferred_element_type=F32)
    tot = jnp.dot(same, g, preferred_element_type=F32)
    qt = (q * jnp.exp(cum)).astype(BF16)
    kt = (k * jnp.exp(-cum)).astype(BF16)
    ke = (k * jnp.exp(tot - cum)).astype(BF16)
    att = lax.dot_general(qt, kt, (((1,), (1,)), ((), ())), preferred_element_type=F32)
    att = jnp.where(tri > 0.0, att, 0.0).astype(BF16)
    o_intra = jnp.dot(att, v, preferred_element_type=F32)
    dk = g.shape[1]
    outs = [None] * n_chunk
    for ci in (range(n_chunk - 1, -1, -1) if reverse else range(n_chunk)):
        r0, r1 = ci * c, (ci + 1) * c
        s = s_ref[...]
        outs[ci] = o_intra[r0:r1] + jnp.dot(qt[r0:r1], s.astype(BF16), preferred_element_type=F32)
        tot_col = jnp.sum(jnp.where(eye, jnp.broadcast_to(tot[r0:r0 + 1], (dk, dk)), 0.0),
                          axis=1, keepdims=True)
        kv = lax.dot_general(ke[r0:r1], v[r0:r1], (((0,), (0,)), ((), ())),
                             preferred_element_type=F32)
        s_ref[...] = s * jnp.exp(tot_col) + kv
    return jnp.concatenate(outs, axis=0)


def _gla_kernel(*refs, reverse):
    if reverse:
        (q_ref, k_ref, v_ref, lr_ref, w_ref, b_ref, of_ref, r_ref, ng_ref, y_ref, s_ref) = refs
    else:
        (q_ref, k_ref, v_ref, lr_ref, w_ref, b_ref, o_ref, s_ref) = refs
    step = pl.program_id(2)

    @pl.when(step == 0)
    def _():
        s_ref[...] = jnp.zeros_like(s_ref)

    c = GLA_CHUNK
    n_head, dk, dv = s_ref.shape
    n_row = q_ref.shape[0]
    z = jnp.dot(lr_ref[...], w_ref[...], preferred_element_type=F32) + b_ref[...]
    g = _log_sigmoid(z) * (1.0 / GLA_TAU)
    row = lax.broadcasted_iota(jnp.int32, (n_row, n_row), 0)
    col = lax.broadcasted_iota(jnp.int32, (n_row, n_row), 1)
    shift = c.bit_length() - 1
    same_b = (row >> shift) == (col >> shift)
    same = same_b.astype(F32)
    tri = jnp.logical_and(same_b, (col >= row) if reverse else (col <= row)).astype(F32)
    eye = (lax.broadcasted_iota(jnp.int32, (dk, dk), 0)
           == lax.broadcasted_iota(jnp.int32, (dk, dk), 1))
    for hh in range(n_head):
        kc = pl.ds(hh * dk, dk)
        vc = pl.ds(hh * dv, dv)
        o = _gla_block(q_ref[:, kc].astype(F32), k_ref[:, kc].astype(F32), v_ref[:, vc],
                       g[:, hh * dk:(hh + 1) * dk], s_ref.at[hh], tri, same, eye, reverse)
        if reverse:
            o = o + of_ref[:, vc]
            o = o * lax.rsqrt(jnp.mean(o * o, axis=-1, keepdims=True) + EPS)
            y_ref[:, vc] = (o * ng_ref[...] * r_ref[:, vc].astype(F32)).astype(y_ref.dtype)
        else:
            o_ref[:, vc] = o


def gla_scan(qkv, lr, r_act, o_fwd, w_pad, bias, norm_g, n_batch, n_ctx_blocks, reverse):
    t = qkv.shape[0]
    dkh = w_pad.shape[1] // GLA_HEADS
    dv = qkv.shape[1] - 2 * w_pad.shape[1]
    dvh = dv // GLA_HEADS
    blk = GLA_BLOCK
    ctx_per_batch = n_ctx_blocks // n_batch
    assert ctx_per_batch == 1
    lat_per_batch = (t // blk - n_ctx_blocks) // n_batch
    n_step = ctx_per_batch + lat_per_batch

    def row_blk(b, s):
        if reverse:
            lat = n_ctx_blocks + lat_per_batch * b + (lat_per_batch - s)
        else:
            lat = n_ctx_blocks + lat_per_batch * b + (s - 1)
        return jnp.where(s == 0, b, lat)

    hp = GLA_HEADS_PER_STEP
    kw, vw = hp * dkh, hp * dvh
    k_off = w_pad.shape[1] // kw
    v_off = 2 * w_pad.shape[1] // vw
    in_specs = [pl.BlockSpec((blk, kw), lambda b, h, s: (row_blk(b, s), h)),
                pl.BlockSpec((blk, kw), lambda b, h, s: (row_blk(b, s), k_off + h)),
                pl.BlockSpec((blk, vw), lambda b, h, s: (row_blk(b, s), v_off + h)),
                pl.BlockSpec((blk, LANES), lambda b, h, s: (row_blk(b, s), 0)),
                pl.BlockSpec((LANES, kw), lambda b, h, s: (0, h)),
                pl.BlockSpec((1, kw), lambda b, h, s: (0, h))]
    args = [qkv, qkv, qkv, lr, w_pad, bias]
    if reverse:
        in_specs += [pl.BlockSpec((blk, vw), lambda b, h, s: (row_blk(b, s), h)),
                     pl.BlockSpec((blk, vw), lambda b, h, s: (row_blk(b, s), h)),
                     pl.BlockSpec((1, dvh), lambda b, h, s: (0, 0))]
        args += [o_fwd, r_act, norm_g]
        out_dtype = BF16
    else:
        out_dtype = F32
    return pl.pallas_call(
        functools.partial(_gla_kernel, reverse=reverse),
        grid=(n_batch, GLA_HEADS // hp, n_step),
        in_specs=in_specs,
        out_specs=pl.BlockSpec((blk, vw), lambda b, h, s: (row_blk(b, s), h)),
        out_shape=jax.ShapeDtypeStruct((t, dv), out_dtype),
        scratch_shapes=[pltpu.VMEM((hp, dkh, dvh), F32)],
        compiler_params=_cparams(32, 3),
        name="gla_bwd" if reverse else "gla_fwd",
    )(*args)


def _gmlp_kernel(u_ref, v_ref, ws_ref, bt_ref, o_ref):
    v = v_ref[...].astype(F32)
    mu = jnp.mean(v, axis=-1, keepdims=True)
    vc = v - mu
    var = jnp.mean(vc * vc, axis=-1, keepdims=True)
    vn = (vc * lax.rsqrt(var + EPS)).astype(BF16)
    n_chunk = v.shape[0] // A_CHUNK
    for gi in range(A_GROUPS):
        wsg = ws_ref[gi].astype(BF16)
        bias = bt_ref[:, gi:gi + 1]
        cols = slice(gi * A_GROUP_W, (gi + 1) * A_GROUP_W)
        for n in range(n_chunk):
            rows = slice(n * A_CHUNK, (n + 1) * A_CHUNK)
            s = jnp.dot(wsg, vn[rows, cols], preferred_element_type=F32) + bias
            o_ref[rows, cols] = (u_ref[rows, cols].astype(F32) * s).astype(o_ref.dtype)


def gmlp(uv, a_ws, a_bias_t):
    t = uv.shape[0]
    aw = uv.shape[1] // 2
    tm = 512
    return pl.pallas_call(
        _gmlp_kernel,
        grid=(t // tm,),
        in_specs=[pl.BlockSpec((tm, aw), lambda i: (i, 0)),
                  pl.BlockSpec((tm, aw), lambda i: (i, 1)),
                  pl.BlockSpec(a_ws.shape, lambda i: (0, 0, 0)),
                  pl.BlockSpec(a_bias_t.shape, lambda i: (0, 0))],
        out_specs=pl.BlockSpec((tm, aw), lambda i: (i, 0)),
        out_shape=jax.ShapeDtypeStruct((t, aw), BF16),
        compiler_params=_cparams(32, 1),
        name="gmlp",
    )(uv, uv, a_ws, a_bias_t)


def _merge_kernel(a_ref, y_ref, ga_ref, gb_ref, pa_ref, pb_ref, o_ref, pab_ref, pbb_ref):
    @pl.when(pl.program_id(1) == 0)
    def _():
        pab_ref[...] = pa_ref[...].astype(BF16)
        pbb_ref[...] = pb_ref[...].astype(BF16)

    ya = jnp.dot(a_ref[...], pab_ref[...], preferred_element_type=F32)
    yb = jnp.dot(y_ref[...], pbb_ref[...], preferred_element_type=F32)
    o_ref[...] = (ga_ref[...].astype(F32) * ya + gb_ref[...].astype(F32) * yb).astype(o_ref.dtype)


def merge_proj(a, yb, gates, p_a, p_b, layer):
    t, d = yb.shape
    tm, tn = ROW_TILE, COL_TILE
    nj = d // tn
    ka, kb = p_a.shape[1], p_b.shape[1]
    return pl.pallas_call(
        _merge_kernel,
        grid=(nj, t // tm),
        in_specs=[pl.BlockSpec((tm, ka), lambda j, i: (i, 0)),
                  pl.BlockSpec((tm, kb), lambda j, i: (i, 0)),
                  pl.BlockSpec((tm, tn), lambda j, i: (i, j)),
                  pl.BlockSpec((tm, tn), lambda j, i: (i, nj + j)),
                  pl.BlockSpec((None, ka, tn), lambda j, i: (layer, 0, j)),
                  pl.BlockSpec((None, kb, tn), lambda j, i: (layer, 0, j))],
        out_specs=pl.BlockSpec((tm, tn), lambda j, i: (i, j)),
        out_shape=jax.ShapeDtypeStruct((t, d), BF16),
        scratch_shapes=[pltpu.VMEM((ka, tn), BF16), pltpu.VMEM((kb, tn), BF16)],
        compiler_params=_cparams(48, 2),
        name="merge_proj",
    )(a, yb, gates, gates, p_a, p_b)


def _out_proj_kernel(*refs, n_ctx_tiles):
    m_ref, w_ref = refs[:2]
    gt_ref, o_ref, wb_ref = refs[-3:]
    i = pl.program_id(1)

    @pl.when(i == 0)
    def _():
        wb_ref[...] = w_ref[...].astype(BF16)

    y = jnp.dot(m_ref[...], wb_ref[...], preferred_element_type=F32)
    o_ref[...] = _stream_tile(refs[2:-3], i, n_ctx_tiles) + gt_ref[0] * y


def out_proj(mrg, w_out, layer, stream, mods, gt_row, mod_idx):
    t, d = mrg.shape
    tm, tn = ROW_TILE, COL_TILE
    specs, arrays, nct = _stream_specs(stream, tm, tn, lambda j, i: i, lambda j, i: j)
    return pl.pallas_call(
        functools.partial(_out_proj_kernel, n_ctx_tiles=nct),
        grid=(d // tn, t // tm),
        in_specs=[pl.BlockSpec((tm, d), lambda j, i: (i, 0)),
                  pl.BlockSpec((None, d, tn), lambda j, i: (layer, 0, j))] + specs + [
                  pl.BlockSpec((1, 1, tn), lambda j, i: (gt_row + mod_idx(i), 0, j))],
        out_specs=pl.BlockSpec((tm, tn), lambda j, i: (i, j)),
        out_shape=jax.ShapeDtypeStruct((t, d), F32),
        scratch_shapes=[pltpu.VMEM((d, tn), BF16)],
        compiler_params=_cparams(48, 2),
        name="out_proj",
    )(mrg, w_out, *arrays, mods)


def _router_kernel(x_ref, g_ref, sh_ref, sc_ref, wr_ref, br_ref,
                   h_ref, idx_ref, gate_ref, rank_ref, cnt_ref, carry_ref):
    @pl.when(pl.program_id(0) == 0)
    def _():
        carry_ref[...] = jnp.zeros_like(carry_ref)

    x = x_ref[...]
    r = lax.rsqrt(jnp.mean(x * x, axis=-1, keepdims=True) + EPS)
    h = (x * r) * g_ref[...]
    h = h * (1.0 + sc_ref[0]) + sh_ref[0]
    n_piece = h.shape[1] // LANES
    for p in range(n_piece):
        h_ref[pl.ds(p, h.shape[0], stride=n_piece), :] = h[:, p * LANES:(p + 1) * LANES]
    logits = jnp.dot(h, wr_ref[...], preferred_element_type=F32) + br_ref[...]
    tm = x.shape[0]
    lane = lax.broadcasted_iota(jnp.int32, (tm, LANES), 1)
    lane_f = lane.astype(F32)
    member = jnp.zeros((tm, LANES), F32)
    vals, sels = [], []
    idx_out = jnp.zeros((tm, LANES), F32)
    for kk in range(TOP_K):
        m = jnp.max(logits, axis=1, keepdims=True)
        ik = jnp.min(jnp.where(logits == m, lane_f, float(LANES)), axis=1, keepdims=True)
        sel = lane_f == ik
        logits = jnp.where(sel, -jnp.inf, logits)
        member = member + sel.astype(F32)
        idx_out = jnp.where(lane == kk, ik, idx_out)
        vals.append(m)
        sels.append(sel)
    es = [jnp.exp(v - vals[0]) for v in vals]
    denom = es[0] + es[1] + es[2] + es[3]
    gate_out = jnp.zeros((tm, LANES), F32)
    for kk in range(TOP_K):
        gate_out = jnp.where(lane == kk, es[kk] / denom, gate_out)
    rr = lax.broadcasted_iota(jnp.int32, (tm, tm), 0)
    cc = lax.broadcasted_iota(jnp.int32, (tm, tm), 1)
    strict = (cc < rr).astype(BF16)
    before = jnp.dot(strict, member.astype(BF16), preferred_element_type=F32) + carry_ref[...]
    rank_out = jnp.zeros((tm, LANES), F32)
    for kk in range(TOP_K):
        rk = jnp.sum(jnp.where(sels[kk], before, 0.0), axis=1, keepdims=True)
        rank_out = jnp.where(lane == kk, rk, rank_out)
    carry_ref[...] = carry_ref[...] + jnp.sum(member, axis=0, keepdims=True)
    idx_ref[...] = idx_out.astype(jnp.int32)
    gate_ref[...] = gate_out
    rank_ref[...] = rank_out.astype(jnp.int32)
    cnt_ref[...] = carry_ref[...]


def router(xa, row0_tiles, gain, mods, sh_row, sc_row, mod_idx, wr_pad, br_pad):
    t, d = xa.shape
    tm = 512
    per = ROW_TILE // tm
    off = row0_tiles * per
    n = t // tm - off
    tl = n * tm
    small = lambda dt: jax.ShapeDtypeStruct((tl, LANES), dt)
    return pl.pallas_call(
        _router_kernel,
        grid=(n,),
        in_specs=[pl.BlockSpec((tm, d), lambda i: (i + off, 0)),
                  pl.BlockSpec((1, d), lambda i: (0, 0)),
                  pl.BlockSpec((1, 1, d), lambda i: (sh_row + mod_idx((i + off) // per), 0, 0)),
                  pl.BlockSpec((1, 1, d), lambda i: (sc_row + mod_idx((i + off) // per), 0, 0)),
                  pl.BlockSpec((d, LANES), lambda i: (0, 0)),
                  pl.BlockSpec((1, LANES), lambda i: (0, 0))],
        out_specs=[pl.BlockSpec((tm * (d // LANES), LANES), lambda i: (i, 0)),
                   pl.BlockSpec((tm, LANES), lambda i: (i, 0)),
                   pl.BlockSpec((tm, LANES), lambda i: (i, 0)),
                   pl.BlockSpec((tm, LANES), lambda i: (i, 0)),
                   pl.BlockSpec((1, LANES), lambda i: (0, 0))],
        out_shape=[jax.ShapeDtypeStruct((tl * (d // LANES), LANES), F32), small(jnp.int32), small(F32),
                   small(jnp.int32), jax.ShapeDtypeStruct((1, LANES), F32)],
        scratch_shapes=[pltpu.VMEM((1, LANES), F32)],
        compiler_params=_cparams(40, 1),
        name="router",
    )(xa, gain.reshape(1, d), mods, mods, wr_pad, br_pad)


DMA_UNROLL = 8


def _row_gather(idx_ref, n_rows, src_ref, dst_ref, sem, run=1, priority=0):
    pitch = dst_ref.shape[0] // n_rows

    def body(g, carry):
        for u in range(DMA_UNROLL):
            r = g * DMA_UNROLL + u
            first = idx_ref[0, 0, r] if run == 1 else pl.multiple_of(idx_ref[0, 0, r], run)
            pltpu.async_copy(src_ref.at[pl.ds(first, run)], dst_ref.at[pl.ds(r * pitch, run)], sem,
                             priority=priority)
        return carry
    lax.fori_loop(0, n_rows // DMA_UNROLL, body, 0)


def _expert_kernel(be_ref, bv_ref, na_ref, cur_ref, nxt_ref, h_ref, w1g_ref, w1l_ref, b1_ref,
                   w2_ref, b2_ref, o_ref, gbuf, xs, act_s, sem, *, nf1):
    b = pl.program_id(0)
    j = pl.program_id(1)
    n_act = na_ref[0]
    active = b < n_act
    rows = EXPERT_ROWS
    n_sub = (bv_ref[b] + EXPERT_SUB - 1) // EXPERT_SUB
    jj = jnp.minimum(j, nf1 - 1)

    n_piece = xs.shape[1] // LANES
    pitch = gbuf.shape[0] // rows
    e = be_ref[jnp.minimum(b, n_act - 1)]
    tf, tn = act_s.shape[2], o_ref.shape[1]
    b1g = b1_ref[e, :, pl.ds(pl.multiple_of(jj * tf, tf), tf)]
    b1l = b1_ref[e, :, pl.ds(pl.multiple_of((nf1 + jj) * tf, tf), tf)]
    b2 = b2_ref[e, :, pl.ds(pl.multiple_of(jnp.maximum(j - nf1, 0) * tn, tn), tn)]

    @pl.when(jnp.logical_and(active, j == 0))
    def _():
        @pl.when(b == 0)
        def _():
            _row_gather(cur_ref, rows, h_ref, gbuf, sem, run=n_piece)

        pltpu.make_async_copy(h_ref.at[pl.ds(0, rows * n_piece)],
                              gbuf.at[pl.ds(0, rows * n_piece)], sem).wait()
        for p in range(n_piece):
            xs[:, p * LANES:(p + 1) * LANES] = gbuf[pl.ds(p, rows, stride=pitch), :].astype(BF16)

        @pl.when(b + 1 < n_act)
        def _():
            _row_gather(nxt_ref, rows, h_ref, gbuf, sem, run=n_piece, priority=1)

    @pl.when(jnp.logical_and(jnp.logical_not(active), j >= nf1))
    def _():
        o_ref[...] = jnp.zeros_like(o_ref)

    for m in range(1, rows // EXPERT_SUB + 1):
        r = m * EXPERT_SUB

        @pl.when(jnp.logical_and(active, jnp.logical_and(j < nf1, n_sub == m)))
        def _():
            x = xs[0:r, :]
            hg = jnp.dot(x, w1g_ref[0].astype(BF16), preferred_element_type=F32) + b1g
            hl = jnp.dot(x, w1l_ref[0].astype(BF16), preferred_element_type=F32) + b1l
            hg = jnp.minimum(hg, SWIGLU_LIMIT)
            hl = jnp.clip(hl, -SWIGLU_LIMIT, SWIGLU_LIMIT)
            act = hg * _sigmoid(SWIGLU_ALPHA * hg) * (hl + 1.0)
            act_s[jj, 0:r, :] = act.astype(BF16)

        @pl.when(jnp.logical_and(active, jnp.logical_and(j >= nf1, n_sub == m)))
        def _():
            a = jnp.concatenate([act_s[t, 0:r, :] for t in range(nf1)], axis=1)
            o_ref[0:r, :] = jnp.dot(a, w2_ref[0].astype(BF16),
                                    preferred_element_type=F32) + b2
            if r < rows:
                o_ref[r:rows, :] = jnp.zeros((rows - r, o_ref.shape[1]), o_ref.dtype)


def experts(h, src_tok, blk_e, blk_valid, n_active, w1, b1, w2, b2, layer):
    _, n_exp, d, f2 = w1.shape
    f = f2 // 2
    tf, tn = EXPERT_FT, EXPERT_NT
    nf1, nf2 = f // tf, d // tn
    n_blk = blk_e.shape[0]
    rows = EXPERT_ROWS

    def eb(b, be, na):
        return be[jnp.minimum(b, na[0] - 1)]

    def w1_idx(b, j, be, na, half):
        last = na[0] - 1
        cur = jnp.logical_and(b < na[0], j < nf1)
        e_next = be[jnp.minimum(jnp.minimum(b, last) + 1, last)]
        return (layer, jnp.where(cur, eb(b, be, na), e_next), 0, half + jnp.where(cur, j, 0))

    def w2_idx(b, j, be, na):
        cur = jnp.logical_and(b < na[0], j >= nf1)
        e_prev = be[jnp.maximum(jnp.minimum(b, na[0]) - 1, 0)]
        return (layer, jnp.where(cur, eb(b, be, na), e_prev), 0, jnp.where(cur, j - nf1, nf2 - 1))

    smem_rows = functools.partial(pl.BlockSpec, (1, 1, rows), memory_space=pltpu.SMEM)
    in_specs = [
        smem_rows(lambda b, j, be, bv, na: (b, 0, 0)),
        smem_rows(lambda b, j, be, bv, na: (jnp.minimum(b + 1, n_blk - 1), 0, 0)),
        pl.BlockSpec(memory_space=pl.ANY),
        pl.BlockSpec((None, 1, d, tf), lambda b, j, be, bv, na: w1_idx(b, j, be, na, 0)),
        pl.BlockSpec((None, 1, d, tf), lambda b, j, be, bv, na: w1_idx(b, j, be, na, nf1)),
        pl.BlockSpec((n_exp, 1, f2), lambda b, j, be, bv, na: (0, 0, 0)),
        pl.BlockSpec((None, 1, f, tn), lambda b, j, be, bv, na: w2_idx(b, j, be, na)),
        pl.BlockSpec((n_exp, 1, d), lambda b, j, be, bv, na: (0, 0, 0)),
    ]
    return pl.pallas_call(
        functools.partial(_expert_kernel, nf1=nf1),
        grid_spec=pltpu.PrefetchScalarGridSpec(
            num_scalar_prefetch=3,
            grid=(n_blk, nf1 + nf2),
            in_specs=in_specs,
            out_specs=pl.BlockSpec((rows, tn), lambda b, j, be, bv, na: (b, jnp.maximum(j - nf1, 0))),
            scratch_shapes=[pltpu.VMEM((rows * (d // LANES + 1), LANES), F32), pltpu.VMEM((rows, d), BF16),
                            pltpu.VMEM((nf1, rows, tf), BF16), pltpu.SemaphoreType.DMA(())]),
        out_shape=jax.ShapeDtypeStruct((n_blk * rows, d), F32),
        compiler_params=_cparams(58, 2),
        name="moe_experts",
    )(blk_e, blk_valid, n_active, src_tok, src_tok, h, w1, w1, b1.reshape(n_exp, 1, f2),
      w2, b2.reshape(n_exp, 1, d))


def _combine_kernel(dest_ref, nxt_ref, y_ref, x_ref, gate_ref, gt_ref, ng_ref, o_ref, buf_ref, sem,
                    *, final_norm):
    i = pl.program_id(0)
    tm = x_ref.shape[0]
    n_rows = tm * TOP_K
    slot = i % 2

    @pl.when(i == 0)
    def _():
        _row_gather(dest_ref, n_rows, y_ref, buf_ref.at[0], sem.at[0])

    @pl.when(i + 1 < pl.num_programs(0))
    def _():
        _row_gather(nxt_ref, n_rows, y_ref, buf_ref.at[1 - slot], sem.at[1 - slot])

    pltpu.make_async_copy(y_ref.at[pl.ds(0, n_rows)], buf_ref.at[slot], sem.at[slot]).wait()
    gate = gate_ref[...]
    acc = jnp.zeros(x_ref.shape, F32)
    for kk in range(TOP_K):
        acc = acc + gate[:, kk:kk + 1] * buf_ref[slot, pl.ds(kk * tm, tm), :]
    xn = x_ref[...] + gt_ref[0] * acc
    if final_norm:
        xn = xn * lax.rsqrt(jnp.mean(xn * xn, axis=-1, keepdims=True) + EPS) * ng_ref[...]
    o_ref[...] = xn


def combine(y_sorted, dest_km, xa, row0_tiles, gate, mods, gt_row, mod_idx, norm_g, final_norm):
    t, d = xa.shape
    tm = 256
    per = ROW_TILE // tm
    off = row0_tiles * per
    n = t // tm - off
    out_rows = n * tm if final_norm else t
    out_off = 0 if final_norm else off
    kernel = functools.partial(_combine_kernel, final_norm=final_norm)
    call = pl.pallas_call(
        kernel,
        grid=(n,),
        in_specs=[pl.BlockSpec((1, 1, TOP_K * tm), lambda i: (i, 0, 0), memory_space=pltpu.SMEM),
                  pl.BlockSpec((1, 1, TOP_K * tm), lambda i: (jnp.minimum(i + 1, n - 1), 0, 0),
                               memory_space=pltpu.SMEM),
                  pl.BlockSpec(memory_space=pl.ANY),
                  pl.BlockSpec((tm, d), lambda i: (i + off, 0)),
                  pl.BlockSpec((tm, LANES), lambda i: (i, 0)),
                  pl.BlockSpec((1, 1, d), lambda i: (gt_row + mod_idx((i + off) // per), 0, 0)),
                  pl.BlockSpec((1, d), lambda i: (0, 0))],
        out_specs=pl.BlockSpec((tm, d), lambda i: (i + out_off, 0)),
        out_shape=jax.ShapeDtypeStruct((out_rows, d), F32),
        scratch_shapes=[pltpu.VMEM((2, TOP_K * tm, d), F32), pltpu.SemaphoreType.DMA((2,))],
        compiler_params=_cparams(40, 1),
        name="moe_combine",
    )
    return call(dest_km, dest_km, y_sorted, xa, gate, mods, norm_g.reshape(1, d))


def _routing_tables(eidx, rank, counts, n_blk):
    t = eidx.shape[0]
    rows = EXPERT_ROWS
    padded = (counts + rows - 1) // rows * rows
    p_ends = jnp.cumsum(padded)
    p_starts = p_ends - padded
    starts = jnp.cumsum(counts) - counts
    dest = p_starts[eidx] + rank
    n_active = (p_ends[-1] // rows).astype(jnp.int32).reshape(1)
    blk_start = jnp.arange(n_blk, dtype=jnp.int32) * rows
    blk_e = jnp.minimum(jnp.searchsorted(p_ends, blk_start, side="right"), N_EXPERTS - 1).astype(jnp.int32)
    blk_valid = jnp.clip(counts[blk_e] - (blk_start - p_starts[blk_e]), 0, rows).astype(jnp.int32)
    order = jnp.argsort(eidx.reshape(-1), stable=True).astype(jnp.int32)
    tok_sorted = order // TOP_K
    within = (blk_start - p_starts[blk_e])[:, None] + jnp.arange(rows, dtype=jnp.int32)[None, :]
    pos = jnp.clip(starts[blk_e][:, None] + within, 0, t * TOP_K - 1)
    src = jnp.where(within < counts[blk_e][:, None], tok_sorted[pos], 0)
    return dest, src.reshape(n_blk, 1, rows).astype(jnp.int32), blk_e, blk_valid, n_active


def moe_ffn(xa, row0_tiles, gain, mods, base_row, mod_idx, wr_pad, br_pad, w1, b1, w2, b2, layer,
            norm_final, final_norm):
    n_b = 8
    h, eidx, gate, rank, cnt = router(xa, row0_tiles, gain, mods, base_row + 3 * n_b,
                                      base_row + 4 * n_b, mod_idx, wr_pad, br_pad)
    tl = eidx.shape[0]
    eidx4 = eidx[:, :TOP_K]
    rank4 = rank[:, :TOP_K]
    counts = cnt[0, :N_EXPERTS].astype(jnp.int32)
    n_blk = (tl * TOP_K + N_EXPERTS * (EXPERT_ROWS - 1)) // EXPERT_ROWS
    dest, src, blk_e, blk_valid, n_active = _routing_tables(eidx4, rank4, counts, n_blk)
    ys = experts(h, src * (w1.shape[2] // LANES), blk_e, blk_valid, n_active, w1, b1, w2, b2, layer)
    tmc = 256
    dest_km = dest.reshape(tl // tmc, tmc, TOP_K).transpose(0, 2, 1).reshape(tl // tmc, 1, TOP_K * tmc)
    return combine(ys, dest_km.astype(jnp.int32), xa, row0_tiles, gate, mods, base_row + 5 * n_b,
                   mod_idx, norm_final, final_norm)


def kernel(x, c, ctx, c_ctx, norm_mix, norm_ffn, w_ada, b_ada, w_in, conv_w, a_ws, a_bias, gla_wf, gla_bf, gla_wb, gla_bb, gla_norm, p_a, p_b, w_out, w_router, b_router, w1, b1, w2, b2, norm_final):
    n_b, seq, d = x.shape
    n_ctx = ctx.shape[1]
    depth = w_ada.shape[0]
    assert n_b * n_ctx == ROW_TILE and seq % ROW_TILE == 0 and n_b < 8
    tiles_per_batch = seq // ROW_TILE
    a_width = a_ws.shape[1] * a_ws.shape[2]
    dk = gla_wf.shape[2]
    dv = d

    def mod_idx(tile):
        return jnp.where(tile == 0, n_b, (tile - 1) // tiles_per_batch)

    xa = (ctx.reshape(n_b * n_ctx, d), x.reshape(n_b * seq, d))
    cond = jnp.zeros((8, d), F32).at[:n_b].set(c).at[n_b].set(c_ctx)
    mods_all = ada_mods(cond, w_ada, b_ada)
    mods = mods_all.reshape(depth, 8, N_MOD, d).transpose(0, 2, 1, 3).reshape(depth * N_MOD * 8, 1, d)

    c_u, c_q, c_r, c_lr, c_g = 0, 2 * a_width, 2 * a_width + 2 * dk + dv, 2 * a_width + 2 * dk + 2 * dv, \
        2 * a_width + 2 * dk + 2 * dv + 2 * GLA_RANK
    w_in_t = jnp.swapaxes(w_in, 1, 2)
    out = None
    for l in range(depth):
        last = l == depth - 1
        base = l * N_MOD * 8
        w_lr = jnp.zeros((1, LANES, d), F32).at[0, :2 * GLA_RANK].set(
            w_in_t[l, c_lr:c_lr + 2 * GLA_RANK])
        w_g = w_in_t[l:l + 1, c_g:]
        wf_pad = jnp.zeros((LANES, dk), F32).at[:GLA_RANK].set(gla_wf[l])
        wb_pad = jnp.zeros((LANES, dk), F32).at[GLA_RANK:2 * GLA_RANK].set(gla_wb[l])

        h = norm_mod(xa, norm_mix[l], mods, base + 0 * 8, base + 1 * 8, mod_idx)
        uv = matmul_act(h, w_in_t, l, c_u, 2 * a_width, act="gelu")
        qkv = matmul_act(h, w_in_t, l, c_q, 2 * dk + dv, act="conv", conv_w=conv_w[l],
                         ctx_seg=n_ctx, q_cols=dk, q_scale=float(dk // GLA_HEADS) ** -0.5)
        r_act = matmul_act(h, w_in_t, l, c_r, dv, act="silu")
        lr = matmul_act(h, w_lr, 0, 0, LANES, act="none", out_dtype=F32, tn=LANES)
        gates = matmul_act(h, w_g, 0, 0, 2 * d, act="sigmoid")

        n_ctx_blocks = n_b * n_ctx // GLA_BLOCK
        o_f = gla_scan(qkv, lr, None, None, wf_pad, gla_bf[l].reshape(1, dk), None,
                       n_b, n_ctx_blocks, reverse=False)
        y_gla = gla_scan(qkv, lr, r_act, o_f, wb_pad, gla_bb[l].reshape(1, dk),
                         gla_norm[l].reshape(1, -1), n_b, n_ctx_blocks, reverse=True)
        a = gmlp(uv, a_ws[l], a_bias[l].T)
        mrg = merge_proj(a, y_gla, gates, p_a, p_b, l)
        xa = out_proj(mrg, w_out, l, xa, mods, base + 2 * 8, mod_idx)

        wr_pad = jnp.zeros((d, LANES), F32).at[:, :N_EXPERTS].set(w_router[l])
        br_pad = jnp.full((1, LANES), -jnp.inf, F32).at[0, :N_EXPERTS].set(b_router[l])
        res = moe_ffn(xa, 1 if last else 0, norm_ffn[l], mods, base, mod_idx, wr_pad, br_pad,
                      w1, b1[l], w2, b2[l], l, norm_final, last)
        if last:
            out = res
        else:
            xa = res
    return out.reshape(n_b, seq, d)
```

```python
import functools

import jax
import jax.numpy as jnp
from jax import lax
from jax.experimental import pallas as pl
from jax.experimental.pallas import tpu as pltpu

F32 = jnp.float32
BF16 = jnp.bfloat16

GRID_W = 64
EPS = 1e-6
N_MOD = 6
A_GROUPS = 8
A_GROUP_W = 128
A_CHUNK = 128
GLA_HEADS = 4
GLA_RANK = 16
GLA_TAU = 16.0
GLA_CHUNK = 64
N_EXPERTS = 32
TOP_K = 4
SWIGLU_LIMIT = 7.0
SWIGLU_ALPHA = 1.702

LANES = 128
ROW_TILE = 1024
COL_TILE = 512
GLA_BLOCK = 256
GLA_HEADS_PER_STEP = 4
EXPERT_ROWS = 1024
EXPERT_SUB = 256
EXPERT_FT = 512
EXPERT_NT = 512
MIB = 1024 * 1024


def _cparams(vmem_mib, n_axes):
    return pltpu.CompilerParams(
        dimension_semantics=("arbitrary",) * n_axes,
        vmem_limit_bytes=int(vmem_mib * MIB))


def _sigmoid(t):
    return 0.5 * (1.0 + jnp.tanh(0.5 * t))


def _silu(t):
    return t * _sigmoid(t)


def _gelu_tanh(t):
    return 0.5 * t * (1.0 + jnp.tanh(0.7978845608028654 * (t + 0.044715 * (t * t * t))))


def _log_sigmoid(z):
    return jnp.minimum(z, 0.0) - jnp.log(1.0 + jnp.exp(-jnp.abs(z)))


def _ada_kernel(c_ref, w_ref, b_ref, o_ref):
    s = _silu(c_ref[...]).astype(BF16)
    w = w_ref[0].astype(BF16)
    o_ref[0] = jnp.dot(s, w, preferred_element_type=F32) + b_ref[0]


def ada_mods(cond, w_ada, b_ada):
    n_layer, d, n = w_ada.shape
    tn = 1024
    return pl.pallas_call(
        _ada_kernel,
        grid=(n_layer, n // tn),
        in_specs=[pl.BlockSpec((8, d), lambda l, j: (0, 0)),
                  pl.BlockSpec((1, d, tn), lambda l, j: (l, 0, j)),
                  pl.BlockSpec((1, 1, tn), lambda l, j: (l, 0, j))],
        out_specs=pl.BlockSpec((1, 8, tn), lambda l, j: (l, 0, j)),
        out_shape=jax.ShapeDtypeStruct((n_layer, 8, n), F32),
        compiler_params=_cparams(32, 2),
        name="ada_mods",
    )(cond, w_ada, b_ada.reshape(n_layer, 1, n))


def _stream_specs(stream, tm, tn, row_of, col_of):
    if not isinstance(stream, tuple):
        return [pl.BlockSpec((tm, tn), lambda *g: (row_of(*g), col_of(*g)))], [stream], 0
    ctx2d, lat2d = stream
    nct = ctx2d.shape[0] // tm
    specs = [pl.BlockSpec((tm, tn), lambda *g: (jnp.minimum(row_of(*g), nct - 1), col_of(*g))),
             pl.BlockSpec((tm, tn), lambda *g: (jnp.maximum(row_of(*g) - nct, 0), col_of(*g)))]
    return specs, [ctx2d, lat2d], nct


def _stream_tile(refs, tile, n_ctx_tiles):
    if len(refs) == 1:
        return refs[0][...]
    return jnp.where(tile < n_ctx_tiles, refs[0][...], refs[1][...])


def _norm_mod_kernel(*refs, n_ctx_tiles):
    g_ref, sh_ref, sc_ref, o_ref = refs[-4:]
    x = _stream_tile(refs[:-4], pl.program_id(0), n_ctx_tiles)
    r = lax.rsqrt(jnp.mean(x * x, axis=-1, keepdims=True) + EPS)
    h = (x * r) * g_ref[...]
    h = h * (1.0 + sc_ref[0]) + sh_ref[0]
    o_ref[...] = h.astype(o_ref.dtype)


def norm_mod(stream, gain, mods, sh_row, sc_row, mod_idx):
    t = sum(s.shape[0] for s in stream) if isinstance(stream, tuple) else stream.shape[0]
    d = gain.shape[0]
    tm = 512
    per = ROW_TILE // tm
    specs, arrays, nct = _stream_specs(stream, tm, d, lambda i: i, lambda i: 0)
    return pl.pallas_call(
        functools.partial(_norm_mod_kernel, n_ctx_tiles=nct),
        grid=(t // tm,),
        in_specs=specs + [
            pl.BlockSpec((1, d), lambda i: (0, 0)),
            pl.BlockSpec((1, 1, d), lambda i: (sh_row + mod_idx(i // per), 0, 0)),
            pl.BlockSpec((1, 1, d), lambda i: (sc_row + mod_idx(i // per), 0, 0))],
        out_specs=pl.BlockSpec((tm, d), lambda i: (i, 0)),
        out_shape=jax.ShapeDtypeStruct((t, d), BF16),
        compiler_params=_cparams(40, 1),
        name="norm_mod",
    )(*arrays, gain.reshape(1, d), mods, mods)


def _mm_kernel(*refs, act, ctx_seg, q_tiles, q_scale):
    if act == "conv":
        x_ref, w_ref, cw_ref, o_ref, wb_ref = refs
    else:
        x_ref, w_ref, o_ref, wb_ref = refs
    j = pl.program_id(0)
    i = pl.program_id(1)

    @pl.when(i == 0)
    def _():
        wb_ref[...] = w_ref[...].T.astype(BF16)

    y = jnp.dot(x_ref[...], wb_ref[...], preferred_element_type=F32)
    if act == "gelu":
        y = _gelu_tanh(y)
    elif act == "silu":
        y = _silu(y.astype(o_ref.dtype))
    elif act == "sigmoid":
        y = _sigmoid(y.astype(o_ref.dtype))
    elif act == "conv":
        seg = jnp.where(i == 0, ctx_seg, GRID_W)
        pos = lax.broadcasted_iota(jnp.int32, y.shape, 0) & (seg - 1)
        prev = jnp.where(pos == 0, 0.0, pltpu.roll(y, 1, 0))
        nxt = jnp.where(pos == seg - 1, 0.0, pltpu.roll(y, y.shape[0] - 1, 0))
        cw = cw_ref[...]
        y = _silu((cw[0:1] * prev + cw[1:2] * y + cw[2:3] * nxt).astype(o_ref.dtype))
        y = y * jnp.where(j < q_tiles, q_scale, 1.0).astype(o_ref.dtype)
    o_ref[...] = y.astype(o_ref.dtype)


def matmul_act(h, w_t, layer, col0, ncols, act="none", out_dtype=BF16, conv_w=None,
               ctx_seg=256, q_cols=0, q_scale=1.0, tn=COL_TILE):
    t, k = h.shape
    tm = ROW_TILE
    off = col0 // tn
    in_specs = [pl.BlockSpec((tm, k), lambda j, i: (i, 0)),
                pl.BlockSpec((None, tn, k), lambda j, i: (layer, j + off, 0))]
    args = [h, w_t]
    if act == "conv":
        in_specs.append(pl.BlockSpec((3, tn), lambda j, i: (0, j)))
        args.append(conv_w)
    return pl.pallas_call(
        functools.partial(_mm_kernel, act=act, ctx_seg=ctx_seg, q_tiles=q_cols // tn,
                          q_scale=q_scale),
        grid=(ncols // tn, t // tm),
        in_specs=in_specs,
        out_specs=pl.BlockSpec((tm, tn), lambda j, i: (i, j)),
        out_shape=jax.ShapeDtypeStruct((t, ncols), out_dtype),
        scratch_shapes=[pltpu.VMEM((k, tn), BF16)],
        compiler_params=_cparams(48, 2),
        name="matmul_" + act,
    )(*args)


def _gla_block(q, k, v, g, s_ref, tri, same, eye, reverse):
    c = GLA_CHUNK
    n_chunk = q.shape[0] // c
    cum = jnp.dot(tri, g, preferred_element_type=F32)
    tot = jnp.dot(same, g, preferred_element_type=F32)
    qt = (q * jnp.exp(cum)).astype(BF16)
    kt = (k * jnp.exp(-cum)).astype(BF16)
    ke = (k * jnp.exp(tot - cum)).astype(BF16)
    att = lax.dot_general(qt, kt, (((1,), (1,)), ((), ())), preferred_element_type=F32)
    att = jnp.where(tri > 0.0, att, 0.0).astype(BF16)
    o_intra = jnp.dot(att, v, preferred_element_type=F32)
    dk = g.shape[1]
    outs = [None] * n_chunk
    for ci in (range(n_chunk - 1, -1, -1) if reverse else range(n_chunk)):
        r0, r1 = ci * c, (ci + 1) * c
        s = s_ref[...]
        outs[ci] = o_intra[r0:r1] + jnp.dot(qt[r0:r1], s.astype(BF16), preferred_element_type=F32)
        tot_col = jnp.sum(jnp.where(eye, jnp.broadcast_to(tot[r0:r0 + 1], (dk, dk)), 0.0),
                          axis=1, keepdims=True)
        kv = lax.dot_general(ke[r0:r1], v[r0:r1], (((0,), (0,)), ((), ())),
                             preferred_element_type=F32)
        s_ref[...] = s * jnp.exp(tot_col) + kv
    return jnp.concatenate(outs, axis=0)


def _gla_kernel(*refs, reverse):
    if reverse:
        (q_ref, k_ref, v_ref, lr_ref, w_ref, b_ref, of_ref, r_ref, ng_ref, y_ref, s_ref) = refs
    else:
        (q_ref, k_ref, v_ref, lr_ref, w_ref, b_ref, o_ref, s_ref) = refs
    step = pl.program_id(2)

    @pl.when(step == 0)
    def _():
        s_ref[...] = jnp.zeros_like(s_ref)

    c = GLA_CHUNK
    n_head, dk, dv = s_ref.shape
    n_row = q_ref.shape[0]
    z = jnp.dot(lr_ref[...], w_ref[...], preferred_element_type=F32) + b_ref[...]
    g = _log_sigmoid(z) * (1.0 / GLA_TAU)
    row = lax.broadcasted_iota(jnp.int32, (n_row, n_row), 0)
    col = lax.broadcasted_iota(jnp.int32, (n_row, n_row), 1)
    shift = c.bit_length() - 1
    same_b = (row >> shift) == (col >> shift)
    same = same_b.astype(F32)
    tri = jnp.logical_and(same_b, (col >= row) if reverse else (col <= row)).astype(F32)
    eye = (lax.broadcasted_iota(jnp.int32, (dk, dk), 0)
           == lax.broadcasted_iota(jnp.int32, (dk, dk), 1))
    for hh in range(n_head):
        kc = pl.ds(hh * dk, dk)
        vc = pl.ds(hh * dv, dv)
        o = _gla_block(q_ref[:, kc].astype(F32), k_ref[:, kc].astype(F32), v_ref[:, vc],
                       g[:, hh * dk:(hh + 1) * dk], s_ref.at[hh], tri, same, eye, reverse)
        if reverse:
            o = o + of_ref[:, vc]
            o = o * lax.rsqrt(jnp.mean(o * o, axis=-1, keepdims=True) + EPS)
            y_ref[:, vc] = (o * ng_ref[...] * r_ref[:, vc].astype(F32)).astype(y_ref.dtype)
        else:
            o_ref[:, vc] = o


def gla_scan(qkv, lr, r_act, o_fwd, w_pad, bias, norm_g, n_batch, n_ctx_blocks, reverse):
    t = qkv.shape[0]
    dkh = w_pad.shape[1] // GLA_HEADS
    dv = qkv.shape[1] - 2 * w_pad.shape[1]
    dvh = dv // GLA_HEADS
    blk = GLA_BLOCK
    ctx_per_batch = n_ctx_blocks // n_batch
    assert ctx_per_batch == 1
    lat_per_batch = (t // blk - n_ctx_blocks) // n_batch
    n_step = ctx_per_batch + lat_per_batch

    def row_blk(b, s):
        if reverse:
            lat = n_ctx_blocks + lat_per_batch * b + (lat_per_batch - s)
        else:
            lat = n_ctx_blocks + lat_per_batch * b + (s - 1)
        return jnp.where(s == 0, b, lat)

    hp = GLA_HEADS_PER_STEP
    kw, vw = hp * dkh, hp * dvh
    k_off = w_pad.shape[1] // kw
    v_off = 2 * w_pad.shape[1] // vw
    in_specs = [pl.BlockSpec((blk, kw), lambda b, h, s: (row_blk(b, s), h)),
                pl.BlockSpec((blk, kw), lambda b, h, s: (row_blk(b, s), k_off + h)),
                pl.BlockSpec((blk, vw), lambda b, h, s: (row_blk(b, s), v_off + h)),
                pl.BlockSpec((blk, LANES), lambda b, h, s: (row_blk(b, s), 0)),
                pl.BlockSpec((LANES, kw), lambda b, h, s: (0, h)),
                pl.BlockSpec((1, kw), lambda b, h, s: (0, h))]
    args = [qkv, qkv, qkv, lr, w_pad, bias]
    if reverse:
        in_specs += [pl.BlockSpec((blk, vw), lambda b, h, s: (row_blk(b, s), h)),
                     pl.BlockSpec((blk, vw), lambda b, h, s: (row_blk(b, s), h)),
                     pl.BlockSpec((1, dvh), lambda b, h, s: (0, 0))]
        args += [o_fwd, r_act, norm_g]
        out_dtype = BF16
    else:
        out_dtype = F32
    return pl.pallas_call(
        functools.partial(_gla_kernel, reverse=reverse),
        grid=(n_batch, GLA_HEADS // hp, n_step),
        in_specs=in_specs,
        out_specs=pl.BlockSpec((blk, vw), lambda b, h, s: (row_blk(b, s), h)),
        out_shape=jax.ShapeDtypeStruct((t, dv), out_dtype),
        scratch_shapes=[pltpu.VMEM((hp, dkh, dvh), F32)],
        compiler_params=_cparams(32, 3),
        name="gla_bwd" if reverse else "gla_fwd",
    )(*args)


def _gmlp_kernel(u_ref, v_ref, ws_ref, bt_ref, o_ref):
    v = v_ref[...].astype(F32)
    mu = jnp.mean(v, axis=-1, keepdims=True)
    vc = v - mu
    var = jnp.mean(vc * vc, axis=-1, keepdims=True)
    vn = (vc * lax.rsqrt(var + EPS)).astype(BF16)
    n_chunk = v.shape[0] // A_CHUNK
    for gi in range(A_GROUPS):
        wsg = ws_ref[gi].astype(BF16)
        bias = bt_ref[:, gi:gi + 1]
        cols = slice(gi * A_GROUP_W, (gi + 1) * A_GROUP_W)
        for n in range(n_chunk):
            rows = slice(n * A_CHUNK, (n + 1) * A_CHUNK)
            s = jnp.dot(wsg, vn[rows, cols], preferred_element_type=F32) + bias
            o_ref[rows, cols] = (u_ref[rows, cols].astype(F32) * s).astype(o_ref.dtype)


def gmlp(uv, a_ws, a_bias_t):
    t = uv.shape[0]
    aw = uv.shape[1] // 2
    tm = 512
    return pl.pallas_call(
        _gmlp_kernel,
        grid=(t // tm,),
        in_specs=[pl.BlockSpec((tm, aw), lambda i: (i, 0)),
                  pl.BlockSpec((tm, aw), lambda i: (i, 1)),
                  pl.BlockSpec(a_ws.shape, lambda i: (0, 0, 0)),
                  pl.BlockSpec(a_bias_t.shape, lambda i: (0, 0))],
        out_specs=pl.BlockSpec((tm, aw), lambda i: (i, 0)),
        out_shape=jax.ShapeDtypeStruct((t, aw), BF16),
        compiler_params=_cparams(32, 1),
        name="gmlp",
    )(uv, uv, a_ws, a_bias_t)


def _merge_kernel(a_ref, y_ref, ga_ref, gb_ref, pa_ref, pb_ref, o_ref, pab_ref, pbb_ref):
    @pl.when(pl.program_id(1) == 0)
    def _():
        pab_ref[...] = pa_ref[...].astype(BF16)
        pbb_ref[...] = pb_ref[...].astype(BF16)

    ya = jnp.dot(a_ref[...], pab_ref[...], preferred_element_type=F32)
    yb = jnp.dot(y_ref[...], pbb_ref[...], preferred_element_type=F32)
    o_ref[...] = (ga_ref[...].astype(F32) * ya + gb_ref[...].astype(F32) * yb).astype(o_ref.dtype)


def merge_proj(a, yb, gates, p_a, p_b, layer):
    t, d = yb.shape
    tm, tn = ROW_TILE, COL_TILE
    nj = d // tn
    ka, kb = p_a.shape[1], p_b.shape[1]
    return pl.pallas_call(
        _merge_kernel,
        grid=(nj, t // tm),
        in_specs=[pl.BlockSpec((tm, ka), lambda j, i: (i, 0)),
                  pl.BlockSpec((tm, kb), lambda j, i: (i, 0)),
                  pl.BlockSpec((tm, tn), lambda j, i: (i, j)),
                  pl.BlockSpec((tm, tn), lambda j, i: (i, nj + j)),
                  pl.BlockSpec((None, ka, tn), lambda j, i: (layer, 0, j)),
                  pl.BlockSpec((None, kb, tn), lambda j, i: (layer, 0, j))],
        out_specs=pl.BlockSpec((tm, tn), lambda j, i: (i, j)),
        out_shape=jax.ShapeDtypeStruct((t, d), BF16),
        scratch_shapes=[pltpu.VMEM((ka, tn), BF16), pltpu.VMEM((kb, tn), BF16)],
        compiler_params=_cparams(48, 2),
        name="merge_proj",
    )(a, yb, gates, gates, p_a, p_b)


def _out_proj_kernel(*refs, n_ctx_tiles):
    m_ref, w_ref = refs[:2]
    gt_ref, o_ref, wb_ref = refs[-3:]
    i = pl.program_id(1)

    @pl.when(i == 0)
    def _():
        wb_ref[...] = w_ref[...].astype(BF16)

    y = jnp.dot(m_ref[...], wb_ref[...], preferred_element_type=F32)
    o_ref[...] = _stream_tile(refs[2:-3], i, n_ctx_tiles) + gt_ref[0] * y


def out_proj(mrg, w_out, layer, stream, mods, gt_row, mod_idx):
    t, d = mrg.shape
    tm, tn = ROW_TILE, COL_TILE
    specs, arrays, nct = _stream_specs(stream, tm, tn, lambda j, i: i, lambda j, i: j)
    return pl.pallas_call(
        functools.partial(_out_proj_kernel, n_ctx_tiles=nct),
        grid=(d // tn, t // tm),
        in_specs=[pl.BlockSpec((tm, d), lambda j, i: (i, 0)),
                  pl.BlockSpec((None, d, tn), lambda j, i: (layer, 0, j))] + specs + [
                  pl.BlockSpec((1, 1, tn), lambda j, i: (gt_row + mod_idx(i), 0, j))],
        out_specs=pl.BlockSpec((tm, tn), lambda j, i: (i, j)),
        out_shape=jax.ShapeDtypeStruct((t, d), F32),
        scratch_shapes=[pltpu.VMEM((d, tn), BF16)],
        compiler_params=_cparams(48, 2),
        name="out_proj",
    )(mrg, w_out, *arrays, mods)


def _router_kernel(x_ref, g_ref, sh_ref, sc_ref, wr_ref, br_ref,
                   h_ref, idx_ref, gate_ref, rank_ref, cnt_ref, carry_ref):
    @pl.when(pl.program_id(0) == 0)
    def _():
        carry_ref[...] = jnp.zeros_like(carry_ref)

    x = x_ref[...]
    r = lax.rsqrt(jnp.mean(x * x, axis=-1, keepdims=True) + EPS)
    h = (x * r) * g_ref[...]
    h = h * (1.0 + sc_ref[0]) + sh_ref[0]
    n_piece = h.shape[1] // LANES
    for p in range(n_piece):
        h_ref[pl.ds(p, h.shape[0], stride=n_piece), :] = h[:, p * LANES:(p + 1) * LANES]
    logits = jnp.dot(h, wr_ref[...], preferred_element_type=F32) + br_ref[...]
    tm = x.shape[0]
    lane = lax.broadcasted_iota(jnp.int32, (tm, LANES), 1)
    lane_f = lane.astype(F32)
    member = jnp.zeros((tm, LANES), F32)
    vals, sels = [], []
    idx_out = jnp.zeros((tm, LANES), F32)
    for kk in range(TOP_K):
        m = jnp.max(logits, axis=1, keepdims=True)
        ik = jnp.min(jnp.where(logits == m, lane_f, float(LANES)), axis=1, keepdims=True)
        sel = lane_f == ik
        logits = jnp.where(sel, -jnp.inf, logits)
        member = member + sel.astype(F32)
        idx_out = jnp.where(lane == kk, ik, idx_out)
        vals.append(m)
        sels.append(sel)
    es = [jnp.exp(v - vals[0]) for v in vals]
    denom = es[0] + es[1] + es[2] + es[3]
    gate_out = jnp.zeros((tm, LANES), F32)
    for kk in range(TOP_K):
        gate_out = jnp.where(lane == kk, es[kk] / denom, gate_out)
    rr = lax.broadcasted_iota(jnp.int32, (tm, tm), 0)
    cc = lax.broadcasted_iota(jnp.int32, (tm, tm), 1)
    strict = (cc < rr).astype(BF16)
    before = jnp.dot(strict, member.astype(BF16), preferred_element_type=F32) + carry_ref[...]
    rank_out = jnp.zeros((tm, LANES), F32)
    for kk in range(TOP_K):
        rk = jnp.sum(jnp.where(sels[kk], before, 0.0), axis=1, keepdims=True)
        rank_out = jnp.where(lane == kk, rk, rank_out)
    carry_ref[...] = carry_ref[...] + jnp.sum(member, axis=0, keepdims=True)
    idx_ref[...] = idx_out.astype(jnp.int32)
    gate_ref[...] = gate_out
    rank_ref[...] = rank_out.astype(jnp.int32)
    cnt_ref[...] = carry_ref[...]


def router(xa, row0_tiles, gain, mods, sh_row, sc_row, mod_idx, wr_pad, br_pad):
    t, d = xa.shape
    tm = 512
    per = ROW_TILE // tm
    off = row0_tiles * per
    n = t // tm - off
    tl = n * tm
    small = lambda dt: jax.ShapeDtypeStruct((tl, LANES), dt)
    return pl.pallas_call(
        _router_kernel,
        grid=(n,),
        in_specs=[pl.BlockSpec((tm, d), lambda i: (i + off, 0)),
                  pl.BlockSpec((1, d), lambda i: (0, 0)),
                  pl.BlockSpec((1, 1, d), lambda i: (sh_row + mod_idx((i + off) // per), 0, 0)),
                  pl.BlockSpec((1, 1, d), lambda i: (sc_row + mod_idx((i + off) // per), 0, 0)),
                  pl.BlockSpec((d, LANES), lambda i: (0, 0)),
                  pl.BlockSpec((1, LANES), lambda i: (0, 0))],
        out_specs=[pl.BlockSpec((tm * (d // LANES), LANES), lambda i: (i, 0)),
                   pl.BlockSpec((tm, LANES), lambda i: (i, 0)),
                   pl.BlockSpec((tm, LANES), lambda i: (i, 0)),
                   pl.BlockSpec((tm, LANES), lambda i: (i, 0)),
                   pl.BlockSpec((1, LANES), lambda i: (0, 0))],
        out_shape=[jax.ShapeDtypeStruct((tl * (d // LANES), LANES), F32), small(jnp.int32), small(F32),
                   small(jnp.int32), jax.ShapeDtypeStruct((1, LANES), F32)],
        scratch_shapes=[pltpu.VMEM((1, LANES), F32)],
        compiler_params=_cparams(40, 1),
        name="router",
    )(xa, gain.reshape(1, d), mods, mods, wr_pad, br_pad)


DMA_UNROLL = 16


def _row_gather(idx_ref, n_rows, src_ref, dst_ref, sem, run=1, priority=0):
    pitch = dst_ref.shape[0] // n_rows

    def body(g, carry):
        for u in range(DMA_UNROLL):
            r = g * DMA_UNROLL + u
            first = idx_ref[0, 0, r] if run == 1 else pl.multiple_of(idx_ref[0, 0, r], run)
            pltpu.async_copy(src_ref.at[pl.ds(first, run)], dst_ref.at[pl.ds(r * pitch, run)], sem,
                             priority=priority)
        return carry
    lax.fori_loop(0, n_rows // DMA_UNROLL, body, 0)


def _expert_kernel(be_ref, bv_ref, na_ref, cur_ref, nxt_ref, h_ref, w1g_ref, w1l_ref, b1_ref,
                   w2_ref, b2_ref, o_ref, gbuf, xs, act_s, sem, *, nf1):
    b = pl.program_id(0)
    j = pl.program_id(1)
    n_act = na_ref[0]
    active = b < n_act
    rows = EXPERT_ROWS
    n_sub = (bv_ref[b] + EXPERT_SUB - 1) // EXPERT_SUB
    jj = jnp.minimum(j, nf1 - 1)

    n_piece = xs.shape[1] // LANES
    pitch = gbuf.shape[0] // rows
    e = be_ref[jnp.minimum(b, n_act - 1)]
    tf, tn = act_s.shape[2], o_ref.shape[1]
    b1g = b1_ref[e, :, pl.ds(pl.multiple_of(jj * tf, tf), tf)]
    b1l = b1_ref[e, :, pl.ds(pl.multiple_of((nf1 + jj) * tf, tf), tf)]
    b2 = b2_ref[e, :, pl.ds(pl.multiple_of(jnp.maximum(j - nf1, 0) * tn, tn), tn)]

    @pl.when(jnp.logical_and(active, j == 0))
    def _():
        @pl.when(b == 0)
        def _():
            _row_gather(cur_ref, rows, h_ref, gbuf, sem, run=n_piece)

        pltpu.make_async_copy(h_ref.at[pl.ds(0, rows * n_piece)],
                              gbuf.at[pl.ds(0, rows * n_piece)], sem).wait()
        for p in range(n_piece):
            xs[:, p * LANES:(p + 1) * LANES] = gbuf[pl.ds(p, rows, stride=pitch), :].astype(BF16)

        @pl.when(b + 1 < n_act)
        def _():
            _row_gather(nxt_ref, rows, h_ref, gbuf, sem, run=n_piece, priority=1)

    @pl.when(jnp.logical_and(jnp.logical_not(active), j >= nf1))
    def _():
        o_ref[...] = jnp.zeros_like(o_ref)

    for m in range(1, rows // EXPERT_SUB + 1):
        r = m * EXPERT_SUB

        @pl.when(jnp.logical_and(active, jnp.logical_and(j < nf1, n_sub == m)))
        def _():
            x = xs[0:r, :]
            hg = jnp.dot(x, w1g_ref[0].astype(BF16), preferred_element_type=F32) + b1g
            hl = jnp.dot(x, w1l_ref[0].astype(BF16), preferred_element_type=F32) + b1l
            hg = jnp.minimum(hg, SWIGLU_LIMIT)
            hl = jnp.clip(hl, -SWIGLU_LIMIT, SWIGLU_LIMIT)
            act = hg * _sigmoid(SWIGLU_ALPHA * hg) * (hl + 1.0)
            act_s[jj, 0:r, :] = act.astype(BF16)

        @pl.when(jnp.logical_and(active, jnp.logical_and(j >= nf1, n_sub == m)))
        def _():
            a = jnp.concatenate([act_s[t, 0:r, :] for t in range(nf1)], axis=1)
            o_ref[0:r, :] = jnp.dot(a, w2_ref[0].astype(BF16),
                                    preferred_element_type=F32) + b2
            if r < rows:
                o_ref[r:rows, :] = jnp.zeros((rows - r, o_ref.shape[1]), o_ref.dtype)


def experts(h, src_tok, blk_e, blk_valid, n_active, w1, b1, w2, b2, layer):
    _, n_exp, d, f2 = w1.shape
    f = f2 // 2
    tf, tn = EXPERT_FT, EXPERT_NT
    nf1, nf2 = f // tf, d // tn
    n_blk = blk_e.shape[0]
    rows = EXPERT_ROWS

    def eb(b, be, na):
        return be[jnp.minimum(b, na[0] - 1)]

    def w1_idx(b, j, be, na, half):
        last = na[0] - 1
        cur = jnp.logical_and(b < na[0], j < nf1)
        e_next = be[jnp.minimum(jnp.minimum(b, last) + 1, last)]
        return (layer, jnp.where(cur, eb(b, be, na), e_next), 0, half + jnp.where(cur, j, 0))

    def w2_idx(b, j, be, na):
        cur = jnp.logical_and(b < na[0], j >= nf1)
        e_prev = be[jnp.maximum(jnp.minimum(b, na[0]) - 1, 0)]
        return (layer, jnp.where(cur, eb(b, be, na), e_prev), 0, jnp.where(cur, j - nf1, nf2 - 1))

    smem_rows = functools.partial(pl.BlockSpec, (1, 1, rows), memory_space=pltpu.SMEM)
    in_specs = [
        smem_rows(lambda b, j, be, bv, na: (b, 0, 0)),
        smem_rows(lambda b, j, be, bv, na: (jnp.minimum(b + 1, n_blk - 1), 0, 0)),
        pl.BlockSpec(memory_space=pl.ANY),
        pl.BlockSpec((None, 1, d, tf), lambda b, j, be, bv, na: w1_idx(b, j, be, na, 0)),
        pl.BlockSpec((None, 1, d, tf), lambda b, j, be, bv, na: w1_idx(b, j, be, na, nf1)),
        pl.BlockSpec((n_exp, 1, f2), lambda b, j, be, bv, na: (0, 0, 0)),
        pl.BlockSpec((None, 1, f, tn), lambda b, j, be, bv, na: w2_idx(b, j, be, na)),
        pl.BlockSpec((n_exp, 1, d), lambda b, j, be, bv, na: (0, 0, 0)),
    ]
    return pl.pallas_call(
        functools.partial(_expert_kernel, nf1=nf1),
        grid_spec=pltpu.PrefetchScalarGridSpec(
            num_scalar_prefetch=3,
            grid=(n_blk, nf1 + nf2),
            in_specs=in_specs,
            out_specs=pl.BlockSpec((rows, tn), lambda b, j, be, bv, na: (b, jnp.maximum(j - nf1, 0))),
            scratch_shapes=[pltpu.VMEM((rows * (d // LANES + 1), LANES), F32), pltpu.VMEM((rows, d), BF16),
                            pltpu.VMEM((nf1, rows, tf), BF16), pltpu.SemaphoreType.DMA(())]),
        out_shape=jax.ShapeDtypeStruct((n_blk * rows, d), F32),
        compiler_params=_cparams(58, 2),
        name="moe_experts",
    )(blk_e, blk_valid, n_active, src_tok, src_tok, h, w1, w1, b1.reshape(n_exp, 1, f2),
      w2, b2.reshape(n_exp, 1, d))


def _combine_kernel(dest_ref, nxt_ref, y_ref, x_ref, gate_ref, gt_ref, ng_ref, o_ref, buf_ref, sem,
                    *, final_norm):
    i = pl.program_id(0)
    tm = x_ref.shape[0]
    n_rows = tm * TOP_K
    slot = i % 2

    @pl.when(i == 0)
    def _():
        _row_gather(dest_ref, n_rows, y_ref, buf_ref.at[0], sem.at[0])

    @pl.when(i + 1 < pl.num_programs(0))
    def _():
        _row_gather(nxt_ref, n_rows, y_ref, buf_ref.at[1 - slot], sem.at[1 - slot])

    pltpu.make_async_copy(y_ref.at[pl.ds(0, n_rows)], buf_ref.at[slot], sem.at[slot]).wait()
    gate = gate_ref[...]
    acc = jnp.zeros(x_ref.shape, F32)
    for kk in range(TOP_K):
        acc = acc + gate[:, kk:kk + 1] * buf_ref[slot, pl.ds(kk * tm, tm), :]
    xn = x_ref[...] + gt_ref[0] * acc
    if final_norm:
        xn = xn * lax.rsqrt(jnp.mean(xn * xn, axis=-1, keepdims=True) + EPS) * ng_ref[...]
    o_ref[...] = xn


def combine(y_sorted, dest_km, xa, row0_tiles, gate, mods, gt_row, mod_idx, norm_g, final_norm):
    t, d = xa.shape
    tm = 256
    per = ROW_TILE // tm
    off = row0_tiles * per
    n = t // tm - off
    out_rows = n * tm if final_norm else t
    out_off = 0 if final_norm else off
    kernel = functools.partial(_combine_kernel, final_norm=final_norm)
    call = pl.pallas_call(
        kernel,
        grid=(n,),
        in_specs=[pl.BlockSpec((1, 1, TOP_K * tm), lambda i: (i, 0, 0), memory_space=pltpu.SMEM),
                  pl.BlockSpec((1, 1, TOP_K * tm), lambda i: (jnp.minimum(i + 1, n - 1), 0, 0),
                               memory_space=pltpu.SMEM),
                  pl.BlockSpec(memory_space=pl.ANY),
                  pl.BlockSpec((tm, d), lambda i: (i + off, 0)),
                  pl.BlockSpec((tm, LANES), lambda i: (i, 0)),
                  pl.BlockSpec((1, 1, d), lambda i: (gt_row + mod_idx((i + off) // per), 0, 0)),
                  pl.BlockSpec((1, d), lambda i: (0, 0))],
        out_specs=pl.BlockSpec((tm, d), lambda i: (i + out_off, 0)),
        out_shape=jax.ShapeDtypeStruct((out_rows, d), F32),
        scratch_shapes=[pltpu.VMEM((2, TOP_K * tm, d), F32), pltpu.SemaphoreType.DMA((2,))],
        compiler_params=_cparams(40, 1),
        name="moe_combine",
    )
    return call(dest_km, dest_km, y_sorted, xa, gate, mods, norm_g.reshape(1, d))


def _routing_tables(eidx, rank, counts, n_blk):
    t = eidx.shape[0]
    rows = EXPERT_ROWS
    padded = (counts + rows - 1) // rows * rows
    p_ends = jnp.cumsum(padded)
    p_starts = p_ends - padded
    starts = jnp.cumsum(counts) - counts
    dest = p_starts[eidx] + rank
    n_active = (p_ends[-1] // rows).astype(jnp.int32).reshape(1)
    blk_start = jnp.arange(n_blk, dtype=jnp.int32) * rows
    blk_e = jnp.minimum(jnp.searchsorted(p_ends, blk_start, side="right"), N_EXPERTS - 1).astype(jnp.int32)
    blk_valid = jnp.clip(counts[blk_e] - (blk_start - p_starts[blk_e]), 0, rows).astype(jnp.int32)
    order = jnp.argsort(eidx.reshape(-1), stable=True).astype(jnp.int32)
    tok_sorted = order // TOP_K
    within = (blk_start - p_starts[blk_e])[:, None] + jnp.arange(rows, dtype=jnp.int32)[None, :]
    pos = jnp.clip(starts[blk_e][:, None] + within, 0, t * TOP_K - 1)
    src = jnp.where(within < counts[blk_e][:, None], tok_sorted[pos], 0)
    return dest, src.reshape(n_blk, 1, rows).astype(jnp.int32), blk_e, blk_valid, n_active


def moe_ffn(xa, row0_tiles, gain, mods, base_row, mod_idx, wr_pad, br_pad, w1, b1, w2, b2, layer,
            norm_final, final_norm):
    n_b = 8
    h, eidx, gate, rank, cnt = router(xa, row0_tiles, gain, mods, base_row + 3 * n_b,
                                      base_row + 4 * n_b, mod_idx, wr_pad, br_pad)
    tl = eidx.shape[0]
    eidx4 = eidx[:, :TOP_K]
    rank4 = rank[:, :TOP_K]
    counts = cnt[0, :N_EXPERTS].astype(jnp.int32)
    n_blk = (tl * TOP_K + N_EXPERTS * (EXPERT_ROWS - 1)) // EXPERT_ROWS
    dest, src, blk_e, blk_valid, n_active = _routing_tables(eidx4, rank4, counts, n_blk)
    ys = experts(h, src * (w1.shape[2] // LANES), blk_e, blk_valid, n_active, w1, b1, w2, b2, layer)
    tmc = 256
    dest_km = dest.reshape(tl // tmc, tmc, TOP_K).transpose(0, 2, 1).reshape(tl // tmc, 1, TOP_K * tmc)
    return combine(ys, dest_km.astype(jnp.int32), xa, row0_tiles, gate, mods, base_row + 5 * n_b,
                   mod_idx, norm_final, final_norm)


def kernel(x, c, ctx, c_ctx, norm_mix, norm_ffn, w_ada, b_ada, w_in, conv_w, a_ws, a_bias, gla_wf, gla_bf, gla_wb, gla_bb, gla_norm, p_a, p_b, w_out, w_router, b_router, w1, b1, w2, b2, norm_final):
    n_b, seq, d = x.shape
    n_ctx = ctx.shape[1]
    depth = w_ada.shape[0]
    assert n_b * n_ctx == ROW_TILE and seq % ROW_TILE == 0 and n_b < 8
    tiles_per_batch = seq // ROW_TILE
    a_width = a_ws.shape[1] * a_ws.shape[2]
    dk = gla_wf.shape[2]
    dv = d

    def mod_idx(tile):
        return jnp.where(tile == 0, n_b, (tile - 1) // tiles_per_batch)

    xa = (ctx.reshape(n_b * n_ctx, d), x.reshape(n_b * seq, d))
    cond = jnp.zeros((8, d), F32).at[:n_b].set(c).at[n_b].set(c_ctx)
    mods_all = ada_mods(cond, w_ada, b_ada)
    mods = mods_all.reshape(depth, 8, N_MOD, d).transpose(0, 2, 1, 3).reshape(depth * N_MOD * 8, 1, d)

    c_u, c_q, c_r, c_lr, c_g = 0, 2 * a_width, 2 * a_width + 2 * dk + dv, 2 * a_width + 2 * dk + 2 * dv, \
        2 * a_width + 2 * dk + 2 * dv + 2 * GLA_RANK
    w_in_t = jnp.swapaxes(w_in, 1, 2)
    out = None
    for l in range(depth):
        last = l == depth - 1
        base = l * N_MOD * 8
        w_lr = jnp.zeros((1, LANES, d), F32).at[0, :2 * GLA_RANK].set(
            w_in_t[l, c_lr:c_lr + 2 * GLA_RANK])
        w_g = w_in_t[l:l + 1, c_g:]
        wf_pad = jnp.zeros((LANES, dk), F32).at[:GLA_RANK].set(gla_wf[l])
        wb_pad = jnp.zeros((LANES, dk), F32).at[GLA_RANK:2 * GLA_RANK].set(gla_wb[l])

        h = norm_mod(xa, norm_mix[l], mods, base + 0 * 8, base + 1 * 8, mod_idx)
        uv = matmul_act(h, w_in_t, l, c_u, 2 * a_width, act="gelu")
        qkv = matmul_act(h, w_in_t, l, c_q, 2 * dk + dv, act="conv", conv_w=conv_w[l],
                         ctx_seg=n_ctx, q_cols=dk, q_scale=float(dk // GLA_HEADS) ** -0.5)
        r_act = matmul_act(h, w_in_t, l, c_r, dv, act="silu")
        lr = matmul_act(h, w_lr, 0, 0, LANES, act="none", out_dtype=F32, tn=LANES)
        gates = matmul_act(h, w_g, 0, 0, 2 * d, act="sigmoid")

        n_ctx_blocks = n_b * n_ctx // GLA_BLOCK
        o_f = gla_scan(qkv, lr, None, None, wf_pad, gla_bf[l].reshape(1, dk), None,
                       n_b, n_ctx_blocks, reverse=False)
        y_gla = gla_scan(qkv, lr, r_act, o_f, wb_pad, gla_bb[l].reshape(1, dk),
                         gla_norm[l].reshape(1, -1), n_b, n_ctx_blocks, reverse=True)
        a = gmlp(uv, a_ws[l], a_bias[l].T)
        mrg = merge_proj(a, y_gla, gates, p_a, p_b, l)
        xa = out_proj(mrg, w_out, l, xa, mods, base + 2 * 8, mod_idx)

        wr_pad = jnp.zeros((d, LANES), F32).at[:, :N_EXPERTS].set(w_router[l])
        br_pad = jnp.full((1, LANES), -jnp.inf, F32).at[0, :N_EXPERTS].set(b_router[l])
        res = moe_ffn(xa, 1 if last else 0, norm_ffn[l], mods, base, mod_idx, wr_pad, br_pad,
                      w1, b1[l], w2, b2[l], l, norm_final, last)
        if last:
            out = res
        else:
            xa = res
    return out.reshape(n_b, seq, d)
```
